```python
import jax, jax.numpy as jnp
from jax import lax
import numpy as np

D_MODEL = 1024
BATCH = 8
SEQ = 2048
DEPTH = 1

MEM_LEN = 256
CONV_W = D_MODEL
CONV_K = 3
SB_HEADS = 8
SB_HEAD_DIM = 128
SB_W = SB_HEADS * SB_HEAD_DIM
Q_BLOCK = 128
X_HEADS = 4
X_HEAD_DIM = D_MODEL // X_HEADS
D_FF = 2816
RMS_EPS = 1e-6
PROJ_SIZES = (CONV_W, CONV_W, CONV_W, SB_W, SB_W, SB_W, D_MODEL, D_MODEL)

kernel_name = 'hybrid_conv_stickbreak_macaron_layer'


def rms_norm(x, g):
    xf = x.astype(jnp.float32)
    y = xf * lax.rsqrt(jnp.mean(xf * xf, axis=-1, keepdims=True) + RMS_EPS)
    return (y * g.astype(jnp.float32)).astype(x.dtype)


def swiglu(x, w_gu, w_down):
    gate, up = jnp.split(x @ w_gu, 2, axis=-1)
    return (jax.nn.silu(gate) * up) @ w_down


def short_conv(x, w):
    c = x.shape[-1]
    return lax.conv_general_dilated(
        x, w[:, None, :].astype(x.dtype), window_strides=(1,), padding=[(CONV_K - 1, 0)],
        dimension_numbers=('NWC', 'WIO', 'NWC'), feature_group_count=c)


def stick_breaking_attention(q, k, v):
    seq = q.shape[2]
    scale = SB_HEAD_DIM ** -0.5
    outs = []
    for blk in range(seq // Q_BLOCK):
        start = blk * Q_BLOCK
        end = start + Q_BLOCK
        z = jnp.einsum('bhqd,bhkd->bhqk', q[:, :, start:end], k[:, :, :end]).astype(jnp.float32) * scale
        t_pos = start + jnp.arange(Q_BLOCK)[:, None]
        s_pos = jnp.arange(end)[None, :]
        causal = s_pos < t_pos
        log_1m_beta = jnp.where(causal, jax.nn.log_sigmoid(-z), 0.0)
        after = lax.cumsum(log_1m_beta, axis=3, reverse=True) - log_1m_beta
        a = jnp.where(causal, jnp.exp(jax.nn.log_sigmoid(z) + after), 0.0)
        outs.append(jnp.einsum('bhqk,bhkd->bhqd', a, v[:, :, :end].astype(jnp.float32)))
    return jnp.concatenate(outs, axis=2).astype(q.dtype)


def memory_cross_attention(hn, mn, w_cq, w_ckv, w_co):
    b, s, _ = hn.shape
    m = mn.shape[1]
    q = (hn @ w_cq).reshape(b, s, X_HEADS, X_HEAD_DIM)
    k, v = jnp.split(mn @ w_ckv, 2, axis=-1)
    k = k.reshape(b, m, X_HEADS, X_HEAD_DIM)
    v = v.reshape(b, m, X_HEADS, X_HEAD_DIM)
    scores = jnp.einsum('bshd,bmhd->bhsm', q, k).astype(jnp.float32) * (X_HEAD_DIM ** -0.5)
    p = jax.nn.softmax(scores, axis=-1)
    o = jnp.einsum('bhsm,bmhd->bshd', p, v.astype(jnp.float32)).astype(hn.dtype)
    return o.reshape(b, s, D_MODEL) @ w_co


def hybrid_mixer(u, w_in, b_gate, conv_w, w_conv_out, w_attn_out, w_o):
    b, s, _ = u.shape
    split_at = np.cumsum(PROJ_SIZES)[:6].tolist()
    cb, cc, cx, q, k, v, gates = jnp.split(u @ w_in, split_at, axis=-1)
    gate_pre = gates + b_gate
    g_conv, g_sb = jnp.split(jax.nn.sigmoid(gate_pre), 2, axis=-1)
    y_conv = cb * short_conv(cc * cx, conv_w)
    to_heads = lambda t: t.reshape(b, s, SB_HEADS, SB_HEAD_DIM).transpose(0, 2, 1, 3)
    y_sb = stick_breaking_attention(to_heads(q), to_heads(k), to_heads(v))
    y_sb = y_sb.transpose(0, 2, 1, 3).reshape(b, s, SB_W)
    merged = g_conv * (y_conv @ w_conv_out) + g_sb * (y_sb @ w_attn_out)
    return merged @ w_o


def setup_inputs(seed: int = 0) -> dict:
    key = jax.random.key(seed)
    ks = jax.random.split(key, 21)
    f32 = jnp.float32

    def dense(k, shape):
        return jax.random.normal(k, shape, f32) * (shape[0] ** -0.5)

    def gain(k):
        return 1.0 + 0.01 * jax.random.normal(k, (D_MODEL,), f32)

    return {
        'x': jax.random.normal(ks[0], (BATCH, SEQ, D_MODEL), f32),
        'mem': jax.random.normal(ks[1], (BATCH, MEM_LEN, D_MODEL), f32),
        'g_ffn1': gain(ks[2]),
        'w_ffn1_gu': dense(ks[3], (D_MODEL, 2 * D_FF)),
        'w_ffn1_down': dense(ks[4], (D_FF, D_MODEL)),
        'g_mix': gain(ks[5]),
        'w_in': dense(ks[6], (D_MODEL, sum(PROJ_SIZES))),
        'b_gate': 0.01 * jax.random.normal(ks[7], (2 * D_MODEL,), f32),
        'conv_w': jax.random.normal(ks[8], (CONV_K, CONV_W), f32) * (CONV_K ** -0.5),
        'w_conv_out': dense(ks[9], (CONV_W, D_MODEL)),
        'w_attn_out': dense(ks[10], (SB_W, D_MODEL)),
        'w_o': dense(ks[11], (D_MODEL, D_MODEL)),
        'g_cross': gain(ks[12]),
        'g_mem': gain(ks[13]),
        'w_cq': dense(ks[14], (D_MODEL, D_MODEL)),
        'w_ckv': dense(ks[15], (D_MODEL, 2 * D_MODEL)),
        'w_co': dense(ks[16], (D_MODEL, D_MODEL)),
        'g_ffn2': gain(ks[17]),
        'w_ffn2_gu': dense(ks[18], (D_MODEL, 2 * D_FF)),
        'w_ffn2_down': dense(ks[19], (D_FF, D_MODEL)),
        'g_final': gain(ks[20]),
    }


def reference(x, mem, g_ffn1, w_ffn1_gu, w_ffn1_down, g_mix, w_in, b_gate, conv_w,
              w_conv_out, w_attn_out, w_o, g_cross, g_mem, w_cq, w_ckv, w_co,
              g_ffn2, w_ffn2_gu, w_ffn2_down, g_final):
    h = x
    for _ in range(DEPTH):
        h = h + 0.5 * swiglu(rms_norm(h, g_ffn1), w_ffn1_gu, w_ffn1_down)
        h = h + hybrid_mixer(rms_norm(h, g_mix), w_in, b_gate, conv_w, w_conv_out, w_attn_out, w_o)
        h = h + memory_cross_attention(rms_norm(h, g_cross), rms_norm(mem, g_mem), w_cq, w_ckv, w_co)
        h = h + 0.5 * swiglu(rms_norm(h, g_ffn2), w_ffn2_gu, w_ffn2_down)
    return rms_norm(h, g_final)
```

```python
import functools

import jax
import jax.numpy as jnp
from jax import lax
from jax.experimental import pallas as pl
from jax.experimental.pallas import tpu as pltpu

F32 = jnp.float32
BF16 = jnp.bfloat16

RMS_EPS = 1e-6
SB_HEADS = 8
SB_HEAD_DIM = 128
X_HEADS = 4
CONV_K = 3
EXP_UNDERFLOW = -104.0

V7X_VMEM_LIMIT = 56 * 1024 * 1024
SUBLANES = 8


def _rms(x, g):
    ms = jnp.mean(x * x, axis=-1, keepdims=True)
    return (x * lax.rsqrt(ms + RMS_EPS)) * g


def _dot(a, b):
    return jnp.dot(a, b, preferred_element_type=F32)


def _dot_nt(a, b):
    return lax.dot_general(a, b, (((1,), (1,)), ((), ())), preferred_element_type=F32)


def _ffn_kernel(x_ref, g_ref, wgu_ref, wd_ref, gf_ref, o_ref, *, d_ff, final_norm):
    x = x_ref[...]
    n = _rms(x, g_ref[...]).astype(BF16)
    gu = _dot(n, wgu_ref[...])
    gate = gu[:, :d_ff]
    up = gu[:, d_ff:]
    a = (gate * jax.nn.sigmoid(gate) * up).astype(BF16)
    y = x + 0.5 * _dot(a, wd_ref[...])
    if final_norm:
        y = _rms(y, gf_ref[...])
    o_ref[...] = y


def _ffn(x2d, g, w_gu, w_down, g_final, *, final_norm, tm, name):
    t, d = x2d.shape
    d_ff = w_down.shape[0]
    const = lambda i: (0, 0)
    return pl.pallas_call(
        functools.partial(_ffn_kernel, d_ff=d_ff, final_norm=final_norm),
        out_shape=jax.ShapeDtypeStruct((t, d), F32),
        grid=(t // tm,),
        in_specs=[
            pl.BlockSpec((tm, d), lambda i: (i, 0)),
            pl.BlockSpec((1, d), const),
            pl.BlockSpec((d, 2 * d_ff), const, pipeline_mode=pl.Buffered(1)),
            pl.BlockSpec((d_ff, d), const, pipeline_mode=pl.Buffered(1)),
            pl.BlockSpec((1, d), const),
        ],
        out_specs=pl.BlockSpec((tm, d), lambda i: (i, 0)),
        compiler_params=pltpu.CompilerParams(
            dimension_semantics=("arbitrary",), vmem_limit_bytes=V7X_VMEM_LIMIT),
        name=name,
    )(x2d, g, w_gu, w_down, g_final)


def _mixer_in_kernel(h_ref, g_ref, win_ref, bg_ref, cw_ref, wco_ref,
                     qkv_ref, yc_ref, gsb_ref, carry_ref, *, d):
    j = pl.program_id(1)

    @pl.when(j == 0)
    def _():
        carry_ref[...] = jnp.zeros_like(carry_ref)

    h = h_ref[0]
    tm = h.shape[0]
    u = _rms(h, g_ref[...]).astype(BF16)
    proj = _dot(u, win_ref[...])
    cb = proj[:, 0 * d:1 * d]
    cc = proj[:, 1 * d:2 * d]
    cx = proj[:, 2 * d:3 * d]
    qkv_ref[0] = proj[:, 3 * d:6 * d].astype(BF16)
    gates = jax.nn.sigmoid(proj[:, 6 * d:8 * d] + bg_ref[...])
    g_conv = gates[:, :d]
    gsb_ref[0] = gates[:, d:]

    p = cc * cx
    prev = carry_ref[...]
    row = lax.broadcasted_iota(jnp.int32, p.shape, 0)
    p1 = pltpu.roll(p, 1, 0)
    p1 = jnp.where(row == 0, prev[SUBLANES - 1:SUBLANES, :], p1)
    p2 = pltpu.roll(p, 2, 0)
    p2 = jnp.where(row == 0, prev[SUBLANES - 2:SUBLANES - 1, :], p2)
    p2 = jnp.where(row == 1, prev[SUBLANES - 1:SUBLANES, :], p2)
    cw = cw_ref[...]
    y_conv = cb * (cw[0:1, :] * p2 + cw[1:2, :] * p1 + cw[2:3, :] * p)
    carry_ref[...] = p[tm - SUBLANES:, :]
    yc_ref[0] = g_conv * _dot(y_conv.astype(BF16), wco_ref[...])


def _mixer_in(h1, g, w_in, b_gate, conv_w, w_conv_out, *, tm):
    b, s, d = h1.shape
    const = lambda bi, j: (0, 0)
    tile = lambda bi, j: (bi, j, 0)
    return pl.pallas_call(
        functools.partial(_mixer_in_kernel, d=d),
        out_shape=(
            jax.ShapeDtypeStruct((b, s, 3 * d), BF16),
            jax.ShapeDtypeStruct((b, s, d), F32),
            jax.ShapeDtypeStruct((b, s, d), F32),
        ),
        grid=(b, s // tm),
        in_specs=[
            pl.BlockSpec((1, tm, d), tile),
            pl.BlockSpec((1, d), const),
            pl.BlockSpec((d, 8 * d), const, pipeline_mode=pl.Buffered(1)),
            pl.BlockSpec((1, 2 * d), const),
            pl.BlockSpec((CONV_K, d), const),
            pl.BlockSpec((d, d), const, pipeline_mode=pl.Buffered(1)),
        ],
        out_specs=(
            pl.BlockSpec((1, tm, 3 * d), tile),
            pl.BlockSpec((1, tm, d), tile),
            pl.BlockSpec((1, tm, d), tile),
        ),
        scratch_shapes=[pltpu.VMEM((SUBLANES, d), F32)],
        compiler_params=pltpu.CompilerParams(
            dimension_semantics=("arbitrary", "arbitrary"), vmem_limit_bytes=V7X_VMEM_LIMIT),
        name="mixer_in",
    )(h1, g, w_in, b_gate, conv_w, w_conv_out)


def _sb_attn_kernel(q_ref, k_ref, v_ref, o_ref, *, blk):
    s = q_ref.shape[1]
    nblk = s // blk
    scale = SB_HEAD_DIM ** -0.5

    row = lax.broadcasted_iota(jnp.int32, (blk, blk), 0)
    col = lax.broadcasted_iota(jnp.int32, (blk, blk), 1)
    causal = col < row
    tri_ones = jnp.concatenate(
        [causal.astype(BF16), jnp.ones((blk, blk), BF16)], axis=1)

    def block(q, j, carry, acc, diagonal):
        off = pl.multiple_of(j * blk, blk)
        kb = k_ref[0, pl.ds(off, blk), :]
        vb = v_ref[0, pl.ds(off, blk), :]
        z = _dot_nt(q, kb) * scale
        sp = jnp.log1p(jnp.exp(-jnp.abs(z)))
        log_beta = jnp.minimum(z, 0.0) - sp
        log_1m = -jnp.maximum(z, 0.0) - sp
        if diagonal:
            log_1m = jnp.where(causal, log_1m, 0.0)
        hi = log_1m.astype(BF16)
        lo = (log_1m - hi.astype(F32)).astype(BF16)
        r = _dot(hi, tri_ones) + _dot(lo, tri_ones)
        after = r[:, :blk] + carry
        a = jnp.exp(log_beta + after)
        if diagonal:
            a = jnp.where(causal, a, 0.0)
        acc = acc + _dot(a.astype(BF16), vb)
        carry = carry + r[:, blk:]
        return carry, acc

    def q_block(i, _):
        qoff = pl.multiple_of(i * blk, blk)
        q = q_ref[0, pl.ds(qoff, blk), :]
        zeros = jnp.zeros((blk, blk), F32)
        carry, acc = block(q, i, zeros, jnp.zeros((blk, SB_HEAD_DIM), F32), True)

        def body(n, c):
            return block(q, i - 1 - n, c[0], c[1], False)

        carry, acc = lax.fori_loop(0, i, body, (carry, acc))
        o_ref[0, pl.ds(qoff, blk), :] = acc.astype(o_ref.dtype)
        return 0

    lax.fori_loop(0, nblk, q_block, 0)


def _sb_attn(qkv, *, blk):
    b, s, d3 = qkv.shape
    d = d3 // 3
    hd = SB_HEAD_DIM
    nh = d // hd
    return pl.pallas_call(
        functools.partial(_sb_attn_kernel, blk=blk),
        out_shape=jax.ShapeDtypeStruct((b, s, d), BF16),
        grid=(b, nh),
        in_specs=[
            pl.BlockSpec((1, s, hd), lambda bi, h: (bi, 0, h)),
            pl.BlockSpec((1, s, hd), lambda bi, h: (bi, 0, nh + h)),
            pl.BlockSpec((1, s, hd), lambda bi, h: (bi, 0, 2 * nh + h)),
        ],
        out_specs=pl.BlockSpec((1, s, hd), lambda bi, h: (bi, 0, h)),
        compiler_params=pltpu.CompilerParams(
            dimension_semantics=("arbitrary", "arbitrary")),
        name="sb_attn",
    )(qkv, qkv, qkv)


def _mem_kv_kernel(m_ref, g_ref, w_ref, o_ref):
    mn = _rms(m_ref[0], g_ref[...]).astype(BF16)
    o_ref[0] = _dot(mn, w_ref[...]).astype(BF16)


def _mem_kv(mem, g, w_ckv):
    b, m, d = mem.shape
    const = lambda bi: (0, 0)
    return pl.pallas_call(
        _mem_kv_kernel,
        out_shape=jax.ShapeDtypeStruct((b, m, 2 * d), BF16),
        grid=(b,),
        in_specs=[
            pl.BlockSpec((1, m, d), lambda bi: (bi, 0, 0)),
            pl.BlockSpec((1, d), const),
            pl.BlockSpec((d, 2 * d), const),
        ],
        out_specs=pl.BlockSpec((1, m, 2 * d), lambda bi: (bi, 0, 0)),
        compiler_params=pltpu.CompilerParams(dimension_semantics=("arbitrary",)),
        name="mem_kv",
    )(mem, g, w_ckv)


def _mix_cross_kernel(h_ref, ysb_ref, yc_ref, gsb_ref, kx_ref, vx_ref,
                      wao_ref, wo_ref, g_ref, wcq_ref, wco_ref, o_ref, *, d):
    hd = d // X_HEADS
    merged = yc_ref[0] + gsb_ref[0] * _dot(ysb_ref[0], wao_ref[...])
    h2 = h_ref[0] + _dot(merged.astype(BF16), wo_ref[...])
    hn = _rms(h2, g_ref[...]).astype(BF16)
    qx = _dot(hn, wcq_ref[...]).astype(BF16)
    outs = []
    for hh in range(X_HEADS):
        sl = slice(hh * hd, (hh + 1) * hd)
        sc = _dot_nt(qx[:, sl], kx_ref[0, :, sl]) * (hd ** -0.5)
        e = jnp.exp(sc - jnp.max(sc, axis=-1, keepdims=True))
        p = e / jnp.sum(e, axis=-1, keepdims=True)
        outs.append(_dot(p.astype(BF16), vx_ref[0, :, sl]).astype(BF16))
    o = jnp.concatenate(outs, axis=1)
    o_ref[0] = h2 + _dot(o, wco_ref[...])


def _mix_cross(h1, y_sb, yc, g_sb, kv, w_attn_out, w_o, g_cross, w_cq, w_co, *, tm):
    b, s, d = h1.shape
    m = kv.shape[1]
    const = lambda bi, j: (0, 0)
    tile = lambda bi, j: (bi, j, 0)
    wspec = pl.BlockSpec((d, d), const, pipeline_mode=pl.Buffered(1))
    return pl.pallas_call(
        functools.partial(_mix_cross_kernel, d=d),
        out_shape=jax.ShapeDtypeStruct((b, s, d), F32),
        grid=(b, s // tm),
        in_specs=[
            pl.BlockSpec((1, tm, d), tile),
            pl.BlockSpec((1, tm, d), tile),
            pl.BlockSpec((1, tm, d), tile),
            pl.BlockSpec((1, tm, d), tile),
            pl.BlockSpec((1, m, d), lambda bi, j: (bi, 0, 0)),
            pl.BlockSpec((1, m, d), lambda bi, j: (bi, 0, 1)),
            wspec, wspec,
            pl.BlockSpec((1, d), const),
            wspec, wspec,
        ],
        out_specs=pl.BlockSpec((1, tm, d), tile),
        compiler_params=pltpu.CompilerParams(
            dimension_semantics=("arbitrary", "arbitrary"), vmem_limit_bytes=V7X_VMEM_LIMIT),
        name="mix_cross",
    )(h1, y_sb, yc, g_sb, kv, kv, w_attn_out, w_o, g_cross, w_cq, w_co)


def kernel(x, mem, g_ffn1, w_ffn1_gu, w_ffn1_down, g_mix, w_in, b_gate, conv_w,
           w_conv_out, w_attn_out, w_o, g_cross, g_mem, w_cq, w_ckv, w_co,
           g_ffn2, w_ffn2_gu, w_ffn2_down, g_final):
    b, s, d = x.shape
    row = lambda v: v.reshape(1, -1).astype(F32)
    wb = lambda w: w.astype(BF16)

    h1 = _ffn(x.reshape(b * s, d), row(g_ffn1), wb(w_ffn1_gu), wb(w_ffn1_down), row(g_final),
              final_norm=False, tm=256, name="ffn1").reshape(b, s, d)
    qkv, yc, g_sb = _mixer_in(h1, row(g_mix), wb(w_in), row(b_gate), conv_w.astype(F32),
                              wb(w_conv_out), tm=256)
    y_sb = _sb_attn(qkv, blk=128)
    kv = _mem_kv(mem, row(g_mem), wb(w_ckv))
    h3 = _mix_cross(h1, y_sb, yc, g_sb, kv, wb(w_attn_out), wb(w_o), row(g_cross),
                    wb(w_cq), wb(w_co), tm=256)
    out = _ffn(h3.reshape(b * s, d), row(g_ffn2), wb(w_ffn2_gu), wb(w_ffn2_down), row(g_final),
               final_norm=True, tm=256, name="ffn2")
    return out.reshape(b, s, d)
```

```python
import functools

import jax
import jax.numpy as jnp
from jax import lax
from jax.experimental import pallas as pl
from jax.experimental.pallas import tpu as pltpu

F32 = jnp.float32
BF16 = jnp.bfloat16

RMS_EPS = 1e-6
SB_HEADS = 8
SB_HEAD_DIM = 128
X_HEADS = 4
CONV_K = 3
EXP_UNDERFLOW = -104.0

V7X_VMEM_LIMIT = 56 * 1024 * 1024
SUBLANES = 8


def _rms(x, g):
    ms = jnp.mean(x * x, axis=-1, keepdims=True)
    return (x * lax.rsqrt(ms + RMS_EPS)) * g


def _dot(a, b):
    return jnp.dot(a, b, preferred_element_type=F32)


def _dot_nt(a, b):
    return lax.dot_general(a, b, (((1,), (1,)), ((), ())), preferred_element_type=F32)


def _ffn_kernel(x_ref, g_ref, wgu_ref, wd_ref, gf_ref, o_ref, *, d_ff, final_norm):
    x = x_ref[...]
    n = _rms(x, g_ref[...]).astype(BF16)
    gu = _dot(n, wgu_ref[...])
    gate = gu[:, :d_ff]
    up = gu[:, d_ff:]
    a = (gate * jax.nn.sigmoid(gate) * up).astype(BF16)
    y = x + 0.5 * _dot(a, wd_ref[...])
    if final_norm:
        y = _rms(y, gf_ref[...])
    o_ref[...] = y


def _ffn(x2d, g, w_gu, w_down, g_final, *, final_norm, tm, name):
    t, d = x2d.shape
    d_ff = w_down.shape[0]
    const = lambda i: (0, 0)
    return pl.pallas_call(
        functools.partial(_ffn_kernel, d_ff=d_ff, final_norm=final_norm),
        out_shape=jax.ShapeDtypeStruct((t, d), F32),
        grid=(t // tm,),
        in_specs=[
            pl.BlockSpec((tm, d), lambda i: (i, 0)),
            pl.BlockSpec((1, d), const),
            pl.BlockSpec((d, 2 * d_ff), const, pipeline_mode=pl.Buffered(1)),
            pl.BlockSpec((d_ff, d), const, pipeline_mode=pl.Buffered(1)),
            pl.BlockSpec((1, d), const),
        ],
        out_specs=pl.BlockSpec((tm, d), lambda i: (i, 0)),
        compiler_params=pltpu.CompilerParams(
            dimension_semantics=("arbitrary",), vmem_limit_bytes=V7X_VMEM_LIMIT),
        name=name,
    )(x2d, g, w_gu, w_down, g_final)


def _mixer_in_kernel(h_ref, g_ref, win_ref, bg_ref, cw_ref, wco_ref,
                     qkv_ref, yc_ref, gsb_ref, carry_ref, *, d):
    j = pl.program_id(1)

    @pl.when(j == 0)
    def _():
        carry_ref[...] = jnp.zeros_like(carry_ref)

    h = h_ref[0]
    tm = h.shape[0]
    u = _rms(h, g_ref[...]).astype(BF16)
    proj = _dot(u, win_ref[...])
    cb = proj[:, 0 * d:1 * d]
    cc = proj[:, 1 * d:2 * d]
    cx = proj[:, 2 * d:3 * d]
    qkv_ref[0] = proj[:, 3 * d:6 * d].astype(BF16)
    gates = jax.nn.sigmoid(proj[:, 6 * d:8 * d] + bg_ref[...])
    g_conv = gates[:, :d]
    gsb_ref[0] = gates[:, d:]

    p = cc * cx
    prev = carry_ref[...]
    row = lax.broadcasted_iota(jnp.int32, p.shape, 0)
    p1 = pltpu.roll(p, 1, 0)
    p1 = jnp.where(row == 0, prev[SUBLANES - 1:SUBLANES, :], p1)
    p2 = pltpu.roll(p, 2, 0)
    p2 = jnp.where(row == 0, prev[SUBLANES - 2:SUBLANES - 1, :], p2)
    p2 = jnp.where(row == 1, prev[SUBLANES - 1:SUBLANES, :], p2)
    cw = cw_ref[...]
    y_conv = cb * (cw[0:1, :] * p2 + cw[1:2, :] * p1 + cw[2:3, :] * p)
    carry_ref[...] = p[tm - SUBLANES:, :]
    yc_ref[0] = g_conv * _dot(y_conv.astype(BF16), wco_ref[...])


def _mixer_in(h1, g, w_in, b_gate, conv_w, w_conv_out, *, tm):
    b, s, d = h1.shape
    const = lambda bi, j: (0, 0)
    tile = lambda bi, j: (bi, j, 0)
    return pl.pallas_call(
        functools.partial(_mixer_in_kernel, d=d),
        out_shape=(
            jax.ShapeDtypeStruct((b, s, 3 * d), BF16),
            jax.ShapeDtypeStruct((b, s, d), F32),
            jax.ShapeDtypeStruct((b, s, d), F32),
        ),
        grid=(b, s // tm),
        in_specs=[
            pl.BlockSpec((1, tm, d), tile),
            pl.BlockSpec((1, d), const),
            pl.BlockSpec((d, 8 * d), const, pipeline_mode=pl.Buffered(1)),
            pl.BlockSpec((1, 2 * d), const),
            pl.BlockSpec((CONV_K, d), const),
            pl.BlockSpec((d, d), const, pipeline_mode=pl.Buffered(1)),
        ],
        out_specs=(
            pl.BlockSpec((1, tm, 3 * d), tile),
            pl.BlockSpec((1, tm, d), tile),
            pl.BlockSpec((1, tm, d), tile),
        ),
        scratch_shapes=[pltpu.VMEM((SUBLANES, d), F32)],
        compiler_params=pltpu.CompilerParams(
            dimension_semantics=("arbitrary", "arbitrary"), vmem_limit_bytes=V7X_VMEM_LIMIT),
        name="mixer_in",
    )(h1, g, w_in, b_gate, conv_w, w_conv_out)


def _sb_attn_kernel(q_ref, k_ref, v_ref, o_ref, carry_ref, acc_ref, *, blk, heads):
    s = q_ref.shape[1]
    nblk = s // blk
    hd = SB_HEAD_DIM
    scale = hd ** -0.5

    row = lax.broadcasted_iota(jnp.int32, (blk, blk), 0)
    col = lax.broadcasted_iota(jnp.int32, (blk, blk), 1)
    causal = col < row
    tri_ones = jnp.concatenate(
        [causal.astype(BF16), jnp.ones((blk, blk), BF16)], axis=1)
    tri_ones2 = jnp.concatenate([tri_ones, tri_ones], axis=0)

    def all_heads(qoff, j, diagonal):
        koff = pl.multiple_of(j * blk, blk)
        cols = [slice(g * hd, (g + 1) * hd) for g in range(heads)]
        zs = [_dot_nt(q_ref[0, pl.ds(qoff, blk), c], k_ref[0, pl.ds(koff, blk), c]) * scale
              for c in cols]
        log_betas, rs = [], []
        for z in zs:
            sp = jnp.log(1.0 + jnp.exp(-jnp.abs(z)))
            log_beta = jnp.minimum(z, 0.0) - sp
            log_1m = log_beta - z
            log_betas.append(log_beta)
            if diagonal:
                log_1m = jnp.where(causal, log_1m, 0.0)
            hi = log_1m.astype(BF16)
            lo = (log_1m - hi.astype(F32)).astype(BF16)
            rs.append(_dot(jnp.concatenate([hi, lo], axis=1), tri_ones2))
        m = None
        for g in range(heads):
            after = rs[g][:, :blk]
            carry = rs[g][:, blk:]
            if not diagonal:
                old = carry_ref[g]
                after = after + old
                carry = carry + old
            a = jnp.exp(log_betas[g] + after)
            if diagonal:
                a = jnp.where(causal, a, 0.0)
            pv = _dot(a.astype(BF16), v_ref[0, pl.ds(koff, blk), cols[g]])
            acc_ref[g] = pv if diagonal else acc_ref[g] + pv
            carry_ref[g] = carry
            m = carry if m is None else jnp.maximum(m, carry)
        return jnp.max(m)

    def q_block(i, _):
        qoff = pl.multiple_of(i * blk, blk)
        top = all_heads(qoff, i, True)

        def cond(st):
            return jnp.logical_and(st[0] >= 0, st[1] >= EXP_UNDERFLOW)

        def body(st):
            return st[0] - 1, all_heads(qoff, st[0], False)

        lax.while_loop(cond, body, (i - 1, top))
        for g in range(heads):
            o_ref[0, pl.ds(qoff, blk), g * hd:(g + 1) * hd] = acc_ref[g].astype(o_ref.dtype)
        return 0

    lax.fori_loop(0, nblk, q_block, 0)


def _sb_attn(qkv, *, blk, heads):
    b, s, d3 = qkv.shape
    d = d3 // 3
    hd = SB_HEAD_DIM
    w = heads * hd
    ng = d // w
    return pl.pallas_call(
        functools.partial(_sb_attn_kernel, blk=blk, heads=heads),
        out_shape=jax.ShapeDtypeStruct((b, s, d), BF16),
        grid=(b, ng),
        in_specs=[
            pl.BlockSpec((1, s, w), lambda bi, h: (bi, 0, h)),
            pl.BlockSpec((1, s, w), lambda bi, h: (bi, 0, ng + h)),
            pl.BlockSpec((1, s, w), lambda bi, h: (bi, 0, 2 * ng + h)),
        ],
        out_specs=pl.BlockSpec((1, s, w), lambda bi, h: (bi, 0, h)),
        scratch_shapes=[
            pltpu.VMEM((heads, blk, blk), F32),
            pltpu.VMEM((heads, blk, hd), F32),
        ],
        compiler_params=pltpu.CompilerParams(
            dimension_semantics=("arbitrary", "arbitrary"), vmem_limit_bytes=V7X_VMEM_LIMIT),
        name="sb_attn",
    )(qkv, qkv, qkv)


def _mem_kv_kernel(m_ref, g_ref, w_ref, o_ref):
    mn = _rms(m_ref[0], g_ref[...]).astype(BF16)
    o_ref[0] = _dot(mn, w_ref[...]).astype(BF16)


def _mem_kv(mem, g, w_ckv):
    b, m, d = mem.shape
    const = lambda bi: (0, 0)
    return pl.pallas_call(
        _mem_kv_kernel,
        out_shape=jax.ShapeDtypeStruct((b, m, 2 * d), BF16),
        grid=(b,),
        in_specs=[
            pl.BlockSpec((1, m, d), lambda bi: (bi, 0, 0)),
            pl.BlockSpec((1, d), const),
            pl.BlockSpec((d, 2 * d), const),
        ],
        out_specs=pl.BlockSpec((1, m, 2 * d), lambda bi: (bi, 0, 0)),
        compiler_params=pltpu.CompilerParams(dimension_semantics=("arbitrary",)),
        name="mem_kv",
    )(mem, g, w_ckv)


def _mix_cross_kernel(h_ref, ysb_ref, yc_ref, gsb_ref, kx_ref, vx_ref,
                      wao_ref, wo_ref, g_ref, wcq_ref, wco_ref, o_ref, *, d):
    hd = d // X_HEADS
    merged = yc_ref[0] + gsb_ref[0] * _dot(ysb_ref[0], wao_ref[...])
    h2 = h_ref[0] + _dot(merged.astype(BF16), wo_ref[...])
    hn = _rms(h2, g_ref[...]).astype(BF16)
    qx = _dot(hn, wcq_ref[...]).astype(BF16)
    outs = []
    for hh in range(X_HEADS):
        sl = slice(hh * hd, (hh + 1) * hd)
        sc = _dot_nt(qx[:, sl], kx_ref[0, :, sl]) * (hd ** -0.5)
        e = jnp.exp(sc - jnp.max(sc, axis=-1, keepdims=True))
        p = e / jnp.sum(e, axis=-1, keepdims=True)
        outs.append(_dot(p.astype(BF16), vx_ref[0, :, sl]).astype(BF16))
    o = jnp.concatenate(outs, axis=1)
    o_ref[0] = h2 + _dot(o, wco_ref[...])


def _mix_cross(h1, y_sb, yc, g_sb, kv, w_attn_out, w_o, g_cross, w_cq, w_co, *, tm):
    b, s, d = h1.shape
    m = kv.shape[1]
    const = lambda bi, j: (0, 0)
    tile = lambda bi, j: (bi, j, 0)
    wspec = pl.BlockSpec((d, d), const, pipeline_mode=pl.Buffered(1))
    return pl.pallas_call(
        functools.partial(_mix_cross_kernel, d=d),
        out_shape=jax.ShapeDtypeStruct((b, s, d), F32),
        grid=(b, s // tm),
        in_specs=[
            pl.BlockSpec((1, tm, d), tile),
            pl.BlockSpec((1, tm, d), tile),
            pl.BlockSpec((1, tm, d), tile),
            pl.BlockSpec((1, tm, d), tile),
            pl.BlockSpec((1, m, d), lambda bi, j: (bi, 0, 0)),
            pl.BlockSpec((1, m, d), lambda bi, j: (bi, 0, 1)),
            wspec, wspec,
            pl.BlockSpec((1, d), const),
            wspec, wspec,
        ],
        out_specs=pl.BlockSpec((1, tm, d), tile),
        compiler_params=pltpu.CompilerParams(
            dimension_semantics=("arbitrary", "arbitrary"), vmem_limit_bytes=V7X_VMEM_LIMIT),
        name="mix_cross",
    )(h1, y_sb, yc, g_sb, kv, kv, w_attn_out, w_o, g_cross, w_cq, w_co)


def kernel(x, mem, g_ffn1, w_ffn1_gu, w_ffn1_down, g_mix, w_in, b_gate, conv_w,
           w_conv_out, w_attn_out, w_o, g_cross, g_mem, w_cq, w_ckv, w_co,
           g_ffn2, w_ffn2_gu, w_ffn2_down, g_final):
    b, s, d = x.shape
    row = lambda v: v.reshape(1, -1).astype(F32)
    wb = lambda w: w.astype(BF16)

    h1 = _ffn(x.reshape(b * s, d), row(g_ffn1), wb(w_ffn1_gu), wb(w_ffn1_down), row(g_final),
              final_norm=False, tm=256, name="ffn1").reshape(b, s, d)
    qkv, yc, g_sb = _mixer_in(h1, row(g_mix), wb(w_in), row(b_gate), conv_w.astype(F32),
                              wb(w_conv_out), tm=256)
    y_sb = _sb_attn(qkv, blk=128, heads=SB_HEADS)
    kv = _mem_kv(mem, row(g_mem), wb(w_ckv))
    h3 = _mix_cross(h1, y_sb, yc, g_sb, kv, wb(w_attn_out), wb(w_o), row(g_cross),
                    wb(w_cq), wb(w_co), tm=256)
    out = _ffn(h3.reshape(b * s, d), row(g_ffn2), wb(w_ffn2_gu), wb(w_ffn2_down), row(g_final),
               final_norm=True, tm=256, name="ffn2")
    return out.reshape(b, s, d)
```

```python
import functools

import jax
import jax.numpy as jnp
from jax import lax
from jax.experimental import pallas as pl
from jax.experimental.pallas import tpu as pltpu

F32 = jnp.float32
BF16 = jnp.bfloat16

RMS_EPS = 1e-6
SB_HEADS = 8
SB_HEAD_DIM = 128
X_HEADS = 4
CONV_K = 3
EXP_UNDERFLOW = -104.0

V7X_VMEM_LIMIT = 56 * 1024 * 1024
SUBLANES = 8


def _rms(x, g):
    ms = jnp.mean(x * x, axis=-1, keepdims=True)
    return (x * lax.rsqrt(ms + RMS_EPS)) * g


def _dot(a, b):
    return jnp.dot(a, b, preferred_element_type=F32)


def _dot_nt(a, b):
    return lax.dot_general(a, b, (((1,), (1,)), ((), ())), preferred_element_type=F32)


def _ffn_kernel(x_ref, g_ref, wgu_ref, wd_ref, gf_ref, o_ref, *, d_ff, final_norm, sub):
    for r in range(x_ref.shape[0] // sub):
        rows = pl.ds(r * sub, sub)
        x = x_ref[rows, :]
        n = _rms(x, g_ref[...]).astype(BF16)
        gu = _dot(n, wgu_ref[...])
        gate = gu[:, :d_ff]
        up = gu[:, d_ff:]
        a = (gate * jax.nn.sigmoid(gate) * up).astype(BF16)
        y = x + 0.5 * _dot(a, wd_ref[...])
        if final_norm:
            y = _rms(y, gf_ref[...])
        o_ref[rows, :] = y


def _ffn(x2d, g, w_gu, w_down, g_final, *, final_norm, tm, sub, name):
    t, d = x2d.shape
    d_ff = w_down.shape[0]
    const = lambda i: (0, 0)
    return pl.pallas_call(
        functools.partial(_ffn_kernel, d_ff=d_ff, final_norm=final_norm, sub=sub),
        out_shape=jax.ShapeDtypeStruct((t, d), F32),
        grid=(t // tm,),
        in_specs=[
            pl.BlockSpec((tm, d), lambda i: (i, 0)),
            pl.BlockSpec((1, d), const),
            pl.BlockSpec((d, 2 * d_ff), const, pipeline_mode=pl.Buffered(1)),
            pl.BlockSpec((d_ff, d), const, pipeline_mode=pl.Buffered(1)),
            pl.BlockSpec((1, d), const),
        ],
        out_specs=pl.BlockSpec((tm, d), lambda i: (i, 0)),
        compiler_params=pltpu.CompilerParams(
            dimension_semantics=("arbitrary",), vmem_limit_bytes=V7X_VMEM_LIMIT),
        name=name,
    )(x2d, g, w_gu, w_down, g_final)


def _mixer_in_kernel(h_ref, g_ref, win_ref, bg_ref, cw_ref, wco_ref,
                     qkv_ref, yc_ref, gsb_ref, carry_ref, *, d, sub):
    j = pl.program_id(1)

    @pl.when(j == 0)
    def _():
        carry_ref[...] = jnp.zeros_like(carry_ref)

    cw = cw_ref[...]
    prev = carry_ref[...]
    row = lax.broadcasted_iota(jnp.int32, (sub, d), 0)
    for r in range(h_ref.shape[1] // sub):
        rows = pl.ds(r * sub, sub)
        u = _rms(h_ref[0, rows, :], g_ref[...]).astype(BF16)
        conv_in = _dot(u, win_ref[:, 0:3 * d])
        qkv_ref[0, rows, :] = _dot(u, win_ref[:, 3 * d:6 * d]).astype(BF16)
        gates = jax.nn.sigmoid(_dot(u, win_ref[:, 6 * d:8 * d]) + bg_ref[...])
        gsb_ref[0, rows, :] = gates[:, d:]

        p = conv_in[:, d:2 * d] * conv_in[:, 2 * d:3 * d]
        p1 = pltpu.roll(p, 1, 0)
        p1 = jnp.where(row == 0, prev[SUBLANES - 1:SUBLANES, :], p1)
        p2 = pltpu.roll(p, 2, 0)
        p2 = jnp.where(row == 0, prev[SUBLANES - 2:SUBLANES - 1, :], p2)
        p2 = jnp.where(row == 1, prev[SUBLANES - 1:SUBLANES, :], p2)
        y_conv = conv_in[:, 0:d] * (cw[0:1, :] * p2 + cw[1:2, :] * p1 + cw[2:3, :] * p)
        prev = p[sub - SUBLANES:, :]
        yc_ref[0, rows, :] = gates[:, :d] * _dot(y_conv.astype(BF16), wco_ref[...])
    carry_ref[...] = prev


def _mixer_in(h1, g, w_in, b_gate, conv_w, w_conv_out, *, tm, sub):
    b, s, d = h1.shape
    const = lambda bi, j: (0, 0)
    tile = lambda bi, j: (bi, j, 0)
    return pl.pallas_call(
        functools.partial(_mixer_in_kernel, d=d, sub=sub),
        out_shape=(
            jax.ShapeDtypeStruct((b, s, 3 * d), BF16),
            jax.ShapeDtypeStruct((b, s, d), F32),
            jax.ShapeDtypeStruct((b, s, d), F32),
        ),
        grid=(b, s // tm),
        in_specs=[
            pl.BlockSpec((1, tm, d), tile),
            pl.BlockSpec((1, d), const),
            pl.BlockSpec((d, 8 * d), const, pipeline_mode=pl.Buffered(1)),
            pl.BlockSpec((1, 2 * d), const),
            pl.BlockSpec((CONV_K, d), const),
            pl.BlockSpec((d, d), const, pipeline_mode=pl.Buffered(1)),
        ],
        out_specs=(
            pl.BlockSpec((1, tm, 3 * d), tile),
            pl.BlockSpec((1, tm, d), tile),
            pl.BlockSpec((1, tm, d), tile),
        ),
        scratch_shapes=[pltpu.VMEM((SUBLANES, d), F32)],
        compiler_params=pltpu.CompilerParams(
            dimension_semantics=("arbitrary", "arbitrary"), vmem_limit_bytes=V7X_VMEM_LIMIT),
        name="mixer_in",
    )(h1, g, w_in, b_gate, conv_w, w_conv_out)


def _sb_attn_kernel(q_ref, k_ref, v_ref, o_ref, carry_ref, acc_ref, *, blk, heads):
    s = q_ref.shape[1]
    nblk = s // blk
    hd = SB_HEAD_DIM
    scale = hd ** -0.5

    row = lax.broadcasted_iota(jnp.int32, (blk, blk), 0)
    col = lax.broadcasted_iota(jnp.int32, (blk, blk), 1)
    causal = col < row
    tri_ones = jnp.concatenate(
        [causal.astype(BF16), jnp.ones((blk, blk), BF16)], axis=1)
    tri_ones2 = jnp.concatenate([tri_ones, tri_ones], axis=0)

    def all_heads(qoff, j, diagonal):
        koff = pl.multiple_of(j * blk, blk)
        cols = [slice(g * hd, (g + 1) * hd) for g in range(heads)]
        zs = [_dot_nt(q_ref[0, pl.ds(qoff, blk), c], k_ref[0, pl.ds(koff, blk), c]) * scale
              for c in cols]
        log_betas, rs = [], []
        for z in zs:
            sp = jnp.log(1.0 + jnp.exp(-jnp.abs(z)))
            log_beta = jnp.minimum(z, 0.0) - sp
            log_1m = log_beta - z
            log_betas.append(log_beta)
            if diagonal:
                log_1m = jnp.where(causal, log_1m, 0.0)
            hi = log_1m.astype(BF16)
            lo = (log_1m - hi.astype(F32)).astype(BF16)
            rs.append(_dot(jnp.concatenate([hi, lo], axis=1), tri_ones2))
        m = None
        for g in range(heads):
            after = rs[g][:, :blk]
            carry = rs[g][:, blk:]
            if not diagonal:
                old = carry_ref[g]
                after = after + old
                carry = carry + old
            a = jnp.exp(log_betas[g] + after)
            if diagonal:
                a = jnp.where(causal, a, 0.0)
            pv = _dot(a.astype(BF16), v_ref[0, pl.ds(koff, blk), cols[g]])
            acc_ref[g] = pv if diagonal else acc_ref[g] + pv
            carry_ref[g] = carry
            m = carry if m is None else jnp.maximum(m, carry)
        return jnp.max(m)

    def q_block(i, _):
        qoff = pl.multiple_of(i * blk, blk)
        top = all_heads(qoff, i, True)

        def cond(st):
            return jnp.logical_and(st[0] >= 0, st[1] >= EXP_UNDERFLOW)

        def body(st):
            return st[0] - 1, all_heads(qoff, st[0], False)

        lax.while_loop(cond, body, (i - 1, top))
        for g in range(heads):
            o_ref[0, pl.ds(qoff, blk), g * hd:(g + 1) * hd] = acc_ref[g].astype(o_ref.dtype)
        return 0

    lax.fori_loop(0, nblk, q_block, 0)


def _sb_attn(qkv, *, blk, heads):
    b, s, d3 = qkv.shape
    d = d3 // 3
    hd = SB_HEAD_DIM
    w = heads * hd
    ng = d // w
    return pl.pallas_call(
        functools.partial(_sb_attn_kernel, blk=blk, heads=heads),
        out_shape=jax.ShapeDtypeStruct((b, s, d), BF16),
        grid=(b, ng),
        in_specs=[
            pl.BlockSpec((1, s, w), lambda bi, h: (bi, 0, h)),
            pl.BlockSpec((1, s, w), lambda bi, h: (bi, 0, ng + h)),
            pl.BlockSpec((1, s, w), lambda bi, h: (bi, 0, 2 * ng + h)),
        ],
        out_specs=pl.BlockSpec((1, s, w), lambda bi, h: (bi, 0, h)),
        scratch_shapes=[
            pltpu.VMEM((heads, blk, blk), F32),
            pltpu.VMEM((heads, blk, hd), F32),
        ],
        compiler_params=pltpu.CompilerParams(
            dimension_semantics=("arbitrary", "arbitrary"), vmem_limit_bytes=V7X_VMEM_LIMIT),
        name="sb_attn",
    )(qkv, qkv, qkv)


def _mem_kv_kernel(m_ref, g_ref, w_ref, o_ref):
    mn = _rms(m_ref[0], g_ref[...]).astype(BF16)
    o_ref[0] = _dot(mn, w_ref[...]).astype(BF16)


def _mem_kv(mem, g, w_ckv):
    b, m, d = mem.shape
    const = lambda bi: (0, 0)
    return pl.pallas_call(
        _mem_kv_kernel,
        out_shape=jax.ShapeDtypeStruct((b, m, 2 * d), BF16),
        grid=(b,),
        in_specs=[
            pl.BlockSpec((1, m, d), lambda bi: (bi, 0, 0)),
            pl.BlockSpec((1, d), const),
            pl.BlockSpec((d, 2 * d), const),
        ],
        out_specs=pl.BlockSpec((1, m, 2 * d), lambda bi: (bi, 0, 0)),
        compiler_params=pltpu.CompilerParams(dimension_semantics=("arbitrary",)),
        name="mem_kv",
    )(mem, g, w_ckv)


def _mix_cross_kernel(h_ref, ysb_ref, yc_ref, gsb_ref, kx_ref, vx_ref,
                      wao_ref, wo_ref, g_ref, wcq_ref, wco_ref, o_ref, *, d, sub):
    hd = d // X_HEADS
    tiles = [pl.ds(r * sub, sub) for r in range(h_ref.shape[1] // sub)]
    heads = [slice(hh * hd, (hh + 1) * hd) for hh in range(X_HEADS)]
    att = [_dot(ysb_ref[0, t, :], wao_ref[...]) for t in tiles]
    merged = [(yc_ref[0, t, :] + gsb_ref[0, t, :] * a).astype(BF16) for t, a in zip(tiles, att)]
    h2 = [h_ref[0, t, :] + _dot(m, wo_ref[...]) for t, m in zip(tiles, merged)]
    hn = [_rms(x, g_ref[...]).astype(BF16) for x in h2]
    qx = [_dot(x, wcq_ref[...]).astype(BF16) for x in hn]
    sc = [[_dot_nt(q[:, sl], kx_ref[0, :, sl]) * (hd ** -0.5) for sl in heads] for q in qx]
    outs = []
    for sc_t in sc:
        o_t = []
        for s_h, sl in zip(sc_t, heads):
            e = jnp.exp(s_h - jnp.max(s_h, axis=-1, keepdims=True))
            p = e / jnp.sum(e, axis=-1, keepdims=True)
            o_t.append(_dot(p.astype(BF16), vx_ref[0, :, sl]).astype(BF16))
        outs.append(jnp.concatenate(o_t, axis=1))
    for t, x, o in zip(tiles, h2, outs):
        o_ref[0, t, :] = x + _dot(o, wco_ref[...])


def _mix_cross(h1, y_sb, yc, g_sb, kv, w_attn_out, w_o, g_cross, w_cq, w_co, *, tm, sub):
    b, s, d = h1.shape
    m = kv.shape[1]
    const = lambda bi, j: (0, 0)
    tile = lambda bi, j: (bi, j, 0)
    wspec = pl.BlockSpec((d, d), const, pipeline_mode=pl.Buffered(1))
    return pl.pallas_call(
        functools.partial(_mix_cross_kernel, d=d, sub=sub),
        out_shape=jax.ShapeDtypeStruct((b, s, d), F32),
        grid=(b, s // tm),
        in_specs=[
            pl.BlockSpec((1, tm, d), tile),
            pl.BlockSpec((1, tm, d), tile),
            pl.BlockSpec((1, tm, d), tile),
            pl.BlockSpec((1, tm, d), tile),
            pl.BlockSpec((1, m, d), lambda bi, j: (bi, 0, 0)),
            pl.BlockSpec((1, m, d), lambda bi, j: (bi, 0, 1)),
            wspec, wspec,
            pl.BlockSpec((1, d), const),
            wspec, wspec,
        ],
        out_specs=pl.BlockSpec((1, tm, d), tile),
        compiler_params=pltpu.CompilerParams(
            dimension_semantics=("arbitrary", "arbitrary"), vmem_limit_bytes=V7X_VMEM_LIMIT),
        name="mix_cross",
    )(h1, y_sb, yc, g_sb, kv, kv, w_attn_out, w_o, g_cross, w_cq, w_co)


def kernel(x, mem, g_ffn1, w_ffn1_gu, w_ffn1_down, g_mix, w_in, b_gate, conv_w,
           w_conv_out, w_attn_out, w_o, g_cross, g_mem, w_cq, w_ckv, w_co,
           g_ffn2, w_ffn2_gu, w_ffn2_down, g_final):
    b, s, d = x.shape
    row = lambda v: v.reshape(1, -1).astype(F32)
    wb = lambda w: w.astype(BF16)

    h1 = _ffn(x.reshape(b * s, d), row(g_ffn1), wb(w_ffn1_gu), wb(w_ffn1_down), row(g_final),
              final_norm=False, tm=512, sub=256, name="ffn1").reshape(b, s, d)
    qkv, yc, g_sb = _mixer_in(h1, row(g_mix), wb(w_in), row(b_gate), conv_w.astype(F32),
                              wb(w_conv_out), tm=512, sub=256)
    y_sb = _sb_attn(qkv, blk=128, heads=SB_HEADS)
    kv = _mem_kv(mem, row(g_mem), wb(w_ckv))
    h3 = _mix_cross(h1, y_sb, yc, g_sb, kv, wb(w_attn_out), wb(w_o), row(g_cross),
                    wb(w_cq), wb(w_co), tm=512, sub=256)
    out = _ffn(h3.reshape(b * s, d), row(g_ffn2), wb(w_ffn2_gu), wb(w_ffn2_down), row(g_final),
               final_norm=True, tm=512, sub=256, name="ffn2")
    return out.reshape(b, s, d)
```

```python
import functools

import jax
import jax.numpy as jnp
from jax import lax
from jax.experimental import pallas as pl
from jax.experimental.pallas import tpu as pltpu

F32 = jnp.float32
BF16 = jnp.bfloat16

RMS_EPS = 1e-6
SB_HEADS = 8
SB_HEAD_DIM = 128
X_HEADS = 4
CONV_K = 3
LOG2_E = 1.4426950408889634
EXP2_UNDERFLOW = -151.0
DEAD_CARRY = -1e30

V7X_VMEM_LIMIT = 56 * 1024 * 1024
SUBLANES = 8


def _rms(x, g):
    ms = jnp.mean(x * x, axis=-1, keepdims=True)
    return (x * lax.rsqrt(ms + RMS_EPS)) * g


def _dot(a, b):
    return jnp.dot(a, b, preferred_element_type=F32)


def _dot_nt(a, b):
    return lax.dot_general(a, b, (((1,), (1,)), ((), ())), preferred_element_type=F32)


def _ffn_kernel(x_ref, g_ref, wgu_ref, wd_ref, gf_ref, o_ref, *, d_ff, final_norm, sub):
    for r in range(x_ref.shape[0] // sub):
        rows = pl.ds(r * sub, sub)
        x = x_ref[rows, :]
        n = _rms(x, g_ref[...]).astype(BF16)
        gu = _dot(n, wgu_ref[...])
        gate = gu[:, :d_ff]
        up = gu[:, d_ff:]
        a = (gate * jax.nn.sigmoid(gate) * up).astype(BF16)
        y = x + 0.5 * _dot(a, wd_ref[...])
        if final_norm:
            y = _rms(y, gf_ref[...])
        o_ref[rows, :] = y


def _ffn(x2d, g, w_gu, w_down, g_final, *, final_norm, tm, sub, name):
    t, d = x2d.shape
    d_ff = w_down.shape[0]
    const = lambda i: (0, 0)
    return pl.pallas_call(
        functools.partial(_ffn_kernel, d_ff=d_ff, final_norm=final_norm, sub=sub),
        out_shape=jax.ShapeDtypeStruct((t, d), F32),
        grid=(t // tm,),
        in_specs=[
            pl.BlockSpec((tm, d), lambda i: (i, 0)),
            pl.BlockSpec((1, d), const),
            pl.BlockSpec((d, 2 * d_ff), const, pipeline_mode=pl.Buffered(1)),
            pl.BlockSpec((d_ff, d), const, pipeline_mode=pl.Buffered(1)),
            pl.BlockSpec((1, d), const),
        ],
        out_specs=pl.BlockSpec((tm, d), lambda i: (i, 0)),
        compiler_params=pltpu.CompilerParams(
            dimension_semantics=("arbitrary",), vmem_limit_bytes=V7X_VMEM_LIMIT),
        name=name,
    )(x2d, g, w_gu, w_down, g_final)


def _mixer_in_kernel(h_ref, g_ref, win_ref, bg_ref, cw_ref, wco_ref,
                     qkv_ref, yc_ref, gsb_ref, carry_ref, *, d, sub):
    j = pl.program_id(1)

    @pl.when(j == 0)
    def _():
        carry_ref[...] = jnp.zeros_like(carry_ref)

    cw = cw_ref[...]
    prev = carry_ref[...]
    row = lax.broadcasted_iota(jnp.int32, (sub, d), 0)
    for r in range(h_ref.shape[1] // sub):
        rows = pl.ds(r * sub, sub)
        u = _rms(h_ref[0, rows, :], g_ref[...]).astype(BF16)
        conv_in = _dot(u, win_ref[:, 0:3 * d])
        qkv_ref[0, rows, :] = _dot(u, win_ref[:, 3 * d:6 * d]).astype(BF16)
        gates = jax.nn.sigmoid(_dot(u, win_ref[:, 6 * d:8 * d]) + bg_ref[...])
        gsb_ref[0, rows, :] = gates[:, d:]

        p = conv_in[:, d:2 * d] * conv_in[:, 2 * d:3 * d]
        p1 = pltpu.roll(p, 1, 0)
        p1 = jnp.where(row == 0, prev[SUBLANES - 1:SUBLANES, :], p1)
        p2 = pltpu.roll(p, 2, 0)
        p2 = jnp.where(row == 0, prev[SUBLANES - 2:SUBLANES - 1, :], p2)
        p2 = jnp.where(row == 1, prev[SUBLANES - 1:SUBLANES, :], p2)
        y_conv = conv_in[:, 0:d] * (cw[0:1, :] * p2 + cw[1:2, :] * p1 + cw[2:3, :] * p)
        prev = p[sub - SUBLANES:, :]
        yc_ref[0, rows, :] = gates[:, :d] * _dot(y_conv.astype(BF16), wco_ref[...])
    carry_ref[...] = prev


def _mixer_in(h1, g, w_in, b_gate, conv_w, w_conv_out, *, tm, sub):
    b, s, d = h1.shape
    const = lambda bi, j: (0, 0)
    tile = lambda bi, j: (bi, j, 0)
    return pl.pallas_call(
        functools.partial(_mixer_in_kernel, d=d, sub=sub),
        out_shape=(
            jax.ShapeDtypeStruct((b, s, 3 * d), BF16),
            jax.ShapeDtypeStruct((b, s, d), F32),
            jax.ShapeDtypeStruct((b, s, d), F32),
        ),
        grid=(b, s // tm),
        in_specs=[
            pl.BlockSpec((1, tm, d), tile),
            pl.BlockSpec((1, d), const),
            pl.BlockSpec((d, 8 * d), const, pipeline_mode=pl.Buffered(1)),
            pl.BlockSpec((1, 2 * d), const),
            pl.BlockSpec((CONV_K, d), const),
            pl.BlockSpec((d, d), const, pipeline_mode=pl.Buffered(1)),
        ],
        out_specs=(
            pl.BlockSpec((1, tm, 3 * d), tile),
            pl.BlockSpec((1, tm, d), tile),
            pl.BlockSpec((1, tm, d), tile),
        ),
        scratch_shapes=[pltpu.VMEM((SUBLANES, d), F32)],
        compiler_params=pltpu.CompilerParams(
            dimension_semantics=("arbitrary", "arbitrary"), vmem_limit_bytes=V7X_VMEM_LIMIT),
        name="mixer_in",
    )(h1, g, w_in, b_gate, conv_w, w_conv_out)


def _sb_attn_kernel(q_ref, k_ref, v_ref, o_ref, carry_ref, acc_ref, *, blk, heads, tiles):
    s = q_ref.shape[1]
    ngroups = s // (blk * tiles)
    hd = SB_HEAD_DIM
    c2 = hd ** -0.5 * LOG2_E

    row = lax.broadcasted_iota(jnp.int32, (blk, blk), 0)
    col = lax.broadcasted_iota(jnp.int32, (blk, blk), 1)
    causal = col < row
    tri_ones = jnp.concatenate(
        [causal.astype(BF16), jnp.ones((blk, blk), BF16)], axis=1)
    tri_ones2 = jnp.concatenate([tri_ones, tri_ones], axis=0)
    cols = [slice(g * hd, (g + 1) * hd) for g in range(heads)]

    def step(i0, n, diagonal):
        qoffs, koffs, spent = [], [], []
        for t in range(tiles):
            j = i0 + t - n
            qoffs.append(pl.multiple_of((i0 + t) * blk, blk))
            koffs.append(pl.multiple_of(jnp.maximum(j, 0) * blk, blk))
            spent.append(j <= 0)
        chains = [(t, g) for t in range(tiles) for g in range(heads)]
        z2s = [_dot_nt(q_ref[0, pl.ds(qoffs[t], blk), cols[g]],
                       k_ref[0, pl.ds(koffs[t], blk), cols[g]]) * c2 for t, g in chains]
        log_betas, rs = [], []
        for z2 in z2s:
            sp = jnp.log2(1.0 + jnp.exp2(-jnp.abs(z2)))
            log_beta = jnp.minimum(z2, 0.0) - sp
            log_1m = log_beta - z2
            log_betas.append(log_beta)
            if diagonal:
                log_1m = jnp.where(causal, log_1m, 0.0)
            hi = log_1m.astype(BF16)
            lo = (log_1m - hi.astype(F32)).astype(BF16)
            rs.append(_dot(jnp.concatenate([hi, lo], axis=1), tri_ones2))
        m = None
        for c, (t, g) in enumerate(chains):
            after = rs[c][:, :blk]
            carry = rs[c][:, blk:]
            if not diagonal:
                old = carry_ref[c]
                after = after + old
                carry = carry + old
            a = jnp.exp2(log_betas[c] + after)
            if diagonal:
                a = jnp.where(causal, a, 0.0)
            pv = _dot(a.astype(BF16), v_ref[0, pl.ds(koffs[t], blk), cols[g]])
            acc_ref[c] = pv if diagonal else acc_ref[c] + pv
            if t < tiles - 1:
                carry = jnp.where(spent[t], DEAD_CARRY, carry)
            carry_ref[c] = carry
            m = carry if m is None else jnp.maximum(m, carry)
        return jnp.max(m)

    def q_group(grp, _):
        i0 = grp * tiles
        top = step(i0, 0, True)

        def cond(st):
            return jnp.logical_and(st[0] <= i0 + tiles - 1, st[1] >= EXP2_UNDERFLOW)

        def body(st):
            return st[0] + 1, step(i0, st[0], False)

        lax.while_loop(cond, body, (1, top))
        for t in range(tiles):
            rows = pl.ds(pl.multiple_of((i0 + t) * blk, blk), blk)
            for g in range(heads):
                o_ref[0, rows, cols[g]] = acc_ref[t * heads + g].astype(o_ref.dtype)
        return 0

    lax.fori_loop(0, ngroups, q_group, 0)


def _sb_attn(qkv, *, blk, heads, tiles):
    b, s, d3 = qkv.shape
    d = d3 // 3
    hd = SB_HEAD_DIM
    w = heads * hd
    ng = d // w
    return pl.pallas_call(
        functools.partial(_sb_attn_kernel, blk=blk, heads=heads, tiles=tiles),
        out_shape=jax.ShapeDtypeStruct((b, s, d), BF16),
        grid=(b, ng),
        in_specs=[
            pl.BlockSpec((1, s, w), lambda bi, h: (bi, 0, h)),
            pl.BlockSpec((1, s, w), lambda bi, h: (bi, 0, ng + h)),
            pl.BlockSpec((1, s, w), lambda bi, h: (bi, 0, 2 * ng + h)),
        ],
        out_specs=pl.BlockSpec((1, s, w), lambda bi, h: (bi, 0, h)),
        scratch_shapes=[
            pltpu.VMEM((tiles * heads, blk, blk), F32),
            pltpu.VMEM((tiles * heads, blk, hd), F32),
        ],
        compiler_params=pltpu.CompilerParams(
            dimension_semantics=("arbitrary", "arbitrary"), vmem_limit_bytes=V7X_VMEM_LIMIT),
        name="sb_attn",
    )(qkv, qkv, qkv)


def _mem_kv_kernel(m_ref, g_ref, w_ref, o_ref):
    mn = _rms(m_ref[0], g_ref[...]).astype(BF16)
    o_ref[0] = _dot(mn, w_ref[...]).astype(BF16)


def _mem_kv(mem, g, w_ckv):
    b, m, d = mem.shape
    const = lambda bi: (0, 0)
    return pl.pallas_call(
        _mem_kv_kernel,
        out_shape=jax.ShapeDtypeStruct((b, m, 2 * d), BF16),
        grid=(b,),
        in_specs=[
            pl.BlockSpec((1, m, d), lambda bi: (bi, 0, 0)),
            pl.BlockSpec((1, d), const),
            pl.BlockSpec((d, 2 * d), const),
        ],
        out_specs=pl.BlockSpec((1, m, 2 * d), lambda bi: (bi, 0, 0)),
        compiler_params=pltpu.CompilerParams(dimension_semantics=("arbitrary",)),
        name="mem_kv",
    )(mem, g, w_ckv)


def _mix_cross_kernel(h_ref, ysb_ref, yc_ref, gsb_ref, kx_ref, vx_ref,
                      wao_ref, wo_ref, g_ref, wcq_ref, wco_ref, o_ref, *, d, sub):
    hd = d // X_HEADS
    tiles = [pl.ds(r * sub, sub) for r in range(h_ref.shape[1] // sub)]
    heads = [slice(hh * hd, (hh + 1) * hd) for hh in range(X_HEADS)]
    att = [_dot(ysb_ref[0, t, :], wao_ref[...]) for t in tiles]
    merged = [(yc_ref[0, t, :] + gsb_ref[0, t, :] * a).astype(BF16) for t, a in zip(tiles, att)]
    h2 = [h_ref[0, t, :] + _dot(m, wo_ref[...]) for t, m in zip(tiles, merged)]
    hn = [_rms(x, g_ref[...]).astype(BF16) for x in h2]
    qx = [_dot(x, wcq_ref[...]).astype(BF16) for x in hn]
    sc = [[_dot_nt(q[:, sl], kx_ref[0, :, sl]) * (hd ** -0.5) for sl in heads] for q in qx]
    outs = []
    for sc_t in sc:
        o_t = []
        for s_h, sl in zip(sc_t, heads):
            e = jnp.exp(s_h - jnp.max(s_h, axis=-1, keepdims=True))
            p = e / jnp.sum(e, axis=-1, keepdims=True)
            o_t.append(_dot(p.astype(BF16), vx_ref[0, :, sl]).astype(BF16))
        outs.append(jnp.concatenate(o_t, axis=1))
    for t, x, o in zip(tiles, h2, outs):
        o_ref[0, t, :] = x + _dot(o, wco_ref[...])


def _mix_cross(h1, y_sb, yc, g_sb, kv, w_attn_out, w_o, g_cross, w_cq, w_co, *, tm, sub):
    b, s, d = h1.shape
    m = kv.shape[1]
    const = lambda bi, j: (0, 0)
    tile = lambda bi, j: (bi, j, 0)
    wspec = pl.BlockSpec((d, d), const, pipeline_mode=pl.Buffered(1))
    return pl.pallas_call(
        functools.partial(_mix_cross_kernel, d=d, sub=sub),
        out_shape=jax.ShapeDtypeStruct((b, s, d), F32),
        grid=(b, s // tm),
        in_specs=[
            pl.BlockSpec((1, tm, d), tile),
            pl.BlockSpec((1, tm, d), tile),
            pl.BlockSpec((1, tm, d), tile),
            pl.BlockSpec((1, tm, d), tile),
            pl.BlockSpec((1, m, d), lambda bi, j: (bi, 0, 0)),
            pl.BlockSpec((1, m, d), lambda bi, j: (bi, 0, 1)),
            wspec, wspec,
            pl.BlockSpec((1, d), const),
            wspec, wspec,
        ],
        out_specs=pl.BlockSpec((1, tm, d), tile),
        compiler_params=pltpu.CompilerParams(
            dimension_semantics=("arbitrary", "arbitrary"), vmem_limit_bytes=V7X_VMEM_LIMIT),
        name="mix_cross",
    )(h1, y_sb, yc, g_sb, kv, kv, w_attn_out, w_o, g_cross, w_cq, w_co)


def kernel(x, mem, g_ffn1, w_ffn1_gu, w_ffn1_down, g_mix, w_in, b_gate, conv_w,
           w_conv_out, w_attn_out, w_o, g_cross, g_mem, w_cq, w_ckv, w_co,
           g_ffn2, w_ffn2_gu, w_ffn2_down, g_final):
    b, s, d = x.shape
    row = lambda v: v.reshape(1, -1).astype(F32)
    wb = lambda w: w.astype(BF16)

    h1 = _ffn(x.reshape(b * s, d), row(g_ffn1), wb(w_ffn1_gu), wb(w_ffn1_down), row(g_final),
              final_norm=False, tm=512, sub=256, name="ffn1").reshape(b, s, d)
    qkv, yc, g_sb = _mixer_in(h1, row(g_mix), wb(w_in), row(b_gate), conv_w.astype(F32),
                              wb(w_conv_out), tm=512, sub=256)
    y_sb = _sb_attn(qkv, blk=128, heads=SB_HEADS, tiles=2)
    kv = _mem_kv(mem, row(g_mem), wb(w_ckv))
    h3 = _mix_cross(h1, y_sb, yc, g_sb, kv, wb(w_attn_out), wb(w_o), row(g_cross),
                    wb(w_cq), wb(w_co), tm=512, sub=256)
    out = _ffn(h3.reshape(b * s, d), row(g_ffn2), wb(w_ffn2_gu), wb(w_ffn2_down), row(g_final),
               final_norm=True, tm=512, sub=256, name="ffn2")
    return out.reshape(b, s, d)
```

```python
import functools

import jax
import jax.numpy as jnp
from jax import lax
from jax.experimental import pallas as pl
from jax.experimental.pallas import tpu as pltpu

F32 = jnp.float32
BF16 = jnp.bfloat16

RMS_EPS = 1e-6
SB_HEADS = 8
SB_HEAD_DIM = 128
X_HEADS = 4
CONV_K = 3
LOG2_E = 1.4426950408889634
EXP2_UNDERFLOW = -151.0
DEAD_CARRY = -1e30

V7X_VMEM_LIMIT = 56 * 1024 * 1024
SUBLANES = 8


def _rms(x, g):
    ms = jnp.mean(x * x, axis=-1, keepdims=True)
    return (x * lax.rsqrt(ms + RMS_EPS)) * g


def _dot(a, b):
    return jnp.dot(a, b, preferred_element_type=F32)


def _dot_nt(a, b):
    return lax.dot_general(a, b, (((1,), (1,)), ((), ())), preferred_element_type=F32)


def _ffn_kernel(x_ref, g_ref, wgu_ref, wd_ref, gf_ref, o_ref, *, d_ff, final_norm, sub):
    for r in range(x_ref.shape[0] // sub):
        rows = pl.ds(r * sub, sub)
        x = x_ref[rows, :]
        n = _rms(x, g_ref[...]).astype(BF16)
        gu = _dot(n, wgu_ref[...])
        gate = gu[:, :d_ff]
        up = gu[:, d_ff:]
        a = (gate * jax.nn.sigmoid(gate) * up).astype(BF16)
        y = x + 0.5 * _dot(a, wd_ref[...])
        if final_norm:
            y = _rms(y, gf_ref[...])
        o_ref[rows, :] = y


def _ffn(x2d, g, w_gu, w_down, g_final, *, final_norm, tm, sub, name):
    t, d = x2d.shape
    d_ff = w_down.shape[0]
    const = lambda i: (0, 0)
    return pl.pallas_call(
        functools.partial(_ffn_kernel, d_ff=d_ff, final_norm=final_norm, sub=sub),
        out_shape=jax.ShapeDtypeStruct((t, d), F32),
        grid=(t // tm,),
        in_specs=[
            pl.BlockSpec((tm, d), lambda i: (i, 0)),
            pl.BlockSpec((1, d), const),
            pl.BlockSpec((d, 2 * d_ff), const, pipeline_mode=pl.Buffered(1)),
            pl.BlockSpec((d_ff, d), const, pipeline_mode=pl.Buffered(1)),
            pl.BlockSpec((1, d), const),
        ],
        out_specs=pl.BlockSpec((tm, d), lambda i: (i, 0)),
        compiler_params=pltpu.CompilerParams(
            dimension_semantics=("arbitrary",), vmem_limit_bytes=V7X_VMEM_LIMIT),
        name=name,
    )(x2d, g, w_gu, w_down, g_final)


def _mixer_in_kernel(h_ref, g_ref, win_ref, bg_ref, cw_ref, wco_ref,
                     qkv_ref, yc_ref, gsb_ref, carry_ref, *, d, sub):
    j = pl.program_id(1)

    @pl.when(j == 0)
    def _():
        carry_ref[...] = jnp.zeros_like(carry_ref)

    cw = cw_ref[...]
    prev = carry_ref[...]
    row = lax.broadcasted_iota(jnp.int32, (sub, d), 0)
    for r in range(h_ref.shape[1] // sub):
        rows = pl.ds(r * sub, sub)
        u = _rms(h_ref[0, rows, :], g_ref[...]).astype(BF16)
        conv_in = _dot(u, win_ref[:, 0:3 * d])
        qkv_ref[0, rows, :] = _dot(u, win_ref[:, 3 * d:6 * d]).astype(BF16)
        gates = jax.nn.sigmoid(_dot(u, win_ref[:, 6 * d:8 * d]) + bg_ref[...])
        gsb_ref[0, rows, :] = gates[:, d:]

        p = conv_in[:, d:2 * d] * conv_in[:, 2 * d:3 * d]
        p1 = pltpu.roll(p, 1, 0)
        p1 = jnp.where(row == 0, prev[SUBLANES - 1:SUBLANES, :], p1)
        p2 = pltpu.roll(p, 2, 0)
        p2 = jnp.where(row == 0, prev[SUBLANES - 2:SUBLANES - 1, :], p2)
        p2 = jnp.where(row == 1, prev[SUBLANES - 1:SUBLANES, :], p2)
        y_conv = conv_in[:, 0:d] * (cw[0:1, :] * p2 + cw[1:2, :] * p1 + cw[2:3, :] * p)
        prev = p[sub - SUBLANES:, :]
        yc_ref[0, rows, :] = gates[:, :d] * _dot(y_conv.astype(BF16), wco_ref[...])
    carry_ref[...] = prev


def _mixer_in(h1, g, w_in, b_gate, conv_w, w_conv_out, *, tm, sub):
    b, s, d = h1.shape
    const = lambda bi, j: (0, 0)
    tile = lambda bi, j: (bi, j, 0)
    return pl.pallas_call(
        functools.partial(_mixer_in_kernel, d=d, sub=sub),
        out_shape=(
            jax.ShapeDtypeStruct((b, s, 3 * d), BF16),
            jax.ShapeDtypeStruct((b, s, d), F32),
            jax.ShapeDtypeStruct((b, s, d), F32),
        ),
        grid=(b, s // tm),
        in_specs=[
            pl.BlockSpec((1, tm, d), tile),
            pl.BlockSpec((1, d), const),
            pl.BlockSpec((d, 8 * d), const, pipeline_mode=pl.Buffered(1)),
            pl.BlockSpec((1, 2 * d), const),
            pl.BlockSpec((CONV_K, d), const),
            pl.BlockSpec((d, d), const, pipeline_mode=pl.Buffered(1)),
        ],
        out_specs=(
            pl.BlockSpec((1, tm, 3 * d), tile),
            pl.BlockSpec((1, tm, d), tile),
            pl.BlockSpec((1, tm, d), tile),
        ),
        scratch_shapes=[pltpu.VMEM((SUBLANES, d), F32)],
        compiler_params=pltpu.CompilerParams(
            dimension_semantics=("arbitrary", "arbitrary"), vmem_limit_bytes=V7X_VMEM_LIMIT),
        name="mixer_in",
    )(h1, g, w_in, b_gate, conv_w, w_conv_out)


def _sb_attn_kernel(q_ref, k_ref, v_ref, o_ref, carry_ref, acc_ref, *, blk, heads, tiles):
    s = q_ref.shape[1]
    ngroups = s // (blk * tiles)
    hd = SB_HEAD_DIM
    c2 = hd ** -0.5 * LOG2_E

    row = lax.broadcasted_iota(jnp.int32, (blk, blk), 0)
    col = lax.broadcasted_iota(jnp.int32, (blk, blk), 1)
    causal = col < row
    tri_ones = jnp.concatenate(
        [causal.astype(BF16), jnp.ones((blk, blk), BF16)], axis=1)
    tri_ones2 = jnp.concatenate([tri_ones, tri_ones], axis=0)
    cols = [slice(g * hd, (g + 1) * hd) for g in range(heads)]
    half = blk // 2

    def step(i0, n, diagonal, nrows):
        qoffs, koffs, spent = [], [], []
        for t in range(tiles):
            j = i0 + t - n
            qoffs.append(pl.multiple_of((i0 + t) * blk, blk))
            koffs.append(pl.multiple_of(jnp.maximum(j, 0) * blk, blk))
            spent.append(j <= 0)
        chains = [(t, g) for t in range(tiles) for g in range(heads)]
        mask = causal[:nrows]
        z2s = [_dot_nt(q_ref[0, pl.ds(qoffs[t], nrows), cols[g]],
                       k_ref[0, pl.ds(koffs[t], blk), cols[g]]) * c2 for t, g in chains]
        log_betas, rs = [], []
        for z2 in z2s:
            sp = jnp.log2(1.0 + jnp.exp2(-jnp.abs(z2)))
            log_beta = jnp.minimum(z2, 0.0) - sp
            log_1m = log_beta - z2
            log_betas.append(log_beta)
            if diagonal:
                log_1m = jnp.where(mask, log_1m, 0.0)
            hi = log_1m.astype(BF16)
            lo = (log_1m - hi.astype(F32)).astype(BF16)
            rs.append(_dot(jnp.concatenate([hi, lo], axis=1), tri_ones2))
        m = None
        for c, (t, g) in enumerate(chains):
            after = rs[c][:, :blk]
            carry = rs[c][:, blk:]
            if not diagonal:
                old = carry_ref[c, 0:nrows, :]
                after = after + old
                carry = carry + old
            a = jnp.exp2(log_betas[c] + after)
            if diagonal:
                a = jnp.where(mask, a, 0.0)
            pv = _dot(a.astype(BF16), v_ref[0, pl.ds(koffs[t], blk), cols[g]])
            acc_ref[c, 0:nrows, :] = pv if diagonal else acc_ref[c, 0:nrows, :] + pv
            if t < tiles - 1:
                carry = jnp.where(spent[t], DEAD_CARRY, carry)
            carry_ref[c, 0:nrows, :] = carry
            m = carry if m is None else jnp.maximum(m, carry)
        if nrows == blk:
            return jnp.max(m[:half]), jnp.max(m[half:])
        return jnp.max(m)

    def q_group(grp, _):
        i0 = grp * tiles
        last = i0 + tiles - 1
        top, bottom = step(i0, 0, True, blk)

        def whole_cond(st):
            return jnp.logical_and(st[0] <= last, st[2] >= EXP2_UNDERFLOW)

        def whole_body(st):
            return (st[0] + 1,) + step(i0, st[0], False, blk)

        n, top, _ = lax.while_loop(whole_cond, whole_body, (1, top, bottom))

        def half_cond(st):
            return jnp.logical_and(st[0] <= last, st[1] >= EXP2_UNDERFLOW)

        def half_body(st):
            return st[0] + 1, step(i0, st[0], False, half)

        lax.while_loop(half_cond, half_body, (n, top))
        for t in range(tiles):
            rows = pl.ds(pl.multiple_of((i0 + t) * blk, blk), blk)
            for g in range(heads):
                o_ref[0, rows, cols[g]] = acc_ref[t * heads + g].astype(o_ref.dtype)
        return 0

    lax.fori_loop(0, ngroups, q_group, 0)


def _sb_attn(qkv, *, blk, heads, tiles):
    b, s, d3 = qkv.shape
    d = d3 // 3
    hd = SB_HEAD_DIM
    w = heads * hd
    ng = d // w
    return pl.pallas_call(
        functools.partial(_sb_attn_kernel, blk=blk, heads=heads, tiles=tiles),
        out_shape=jax.ShapeDtypeStruct((b, s, d), BF16),
        grid=(b, ng),
        in_specs=[
            pl.BlockSpec((1, s, w), lambda bi, h: (bi, 0, h)),
            pl.BlockSpec((1, s, w), lambda bi, h: (bi, 0, ng + h)),
            pl.BlockSpec((1, s, w), lambda bi, h: (bi, 0, 2 * ng + h)),
        ],
        out_specs=pl.BlockSpec((1, s, w), lambda bi, h: (bi, 0, h)),
        scratch_shapes=[
            pltpu.VMEM((tiles * heads, blk, blk), F32),
            pltpu.VMEM((tiles * heads, blk, hd), F32),
        ],
        compiler_params=pltpu.CompilerParams(
            dimension_semantics=("arbitrary", "arbitrary"), vmem_limit_bytes=V7X_VMEM_LIMIT),
        name="sb_attn",
    )(qkv, qkv, qkv)


def _mem_kv_kernel(m_ref, g_ref, w_ref, o_ref):
    mn = _rms(m_ref[...], g_ref[...]).astype(BF16)
    o_ref[...] = _dot(mn, w_ref[...]).astype(BF16)


def _mem_kv(mem, g, w_ckv, *, tm):
    b, m, d = mem.shape
    const = lambda i: (0, 0)
    kv = pl.pallas_call(
        _mem_kv_kernel,
        out_shape=jax.ShapeDtypeStruct((b * m, 2 * d), BF16),
        grid=(b * m // tm,),
        in_specs=[
            pl.BlockSpec((tm, d), lambda i: (i, 0)),
            pl.BlockSpec((1, d), const),
            pl.BlockSpec((d, 2 * d), const, pipeline_mode=pl.Buffered(1)),
        ],
        out_specs=pl.BlockSpec((tm, 2 * d), lambda i: (i, 0)),
        compiler_params=pltpu.CompilerParams(
            dimension_semantics=("arbitrary",), vmem_limit_bytes=V7X_VMEM_LIMIT),
        name="mem_kv",
    )(mem.reshape(b * m, d), g, w_ckv)
    return kv.reshape(b, m, 2 * d)


def _mix_cross_kernel(h_ref, ysb_ref, yc_ref, gsb_ref, kx_ref, vx_ref,
                      wao_ref, wo_ref, g_ref, wcq_ref, wco_ref, o_ref, *, d, sub):
    hd = d // X_HEADS
    tiles = [pl.ds(r * sub, sub) for r in range(h_ref.shape[1] // sub)]
    heads = [slice(hh * hd, (hh + 1) * hd) for hh in range(X_HEADS)]
    att = [_dot(ysb_ref[0, t, :], wao_ref[...]) for t in tiles]
    merged = [(yc_ref[0, t, :] + gsb_ref[0, t, :] * a).astype(BF16) for t, a in zip(tiles, att)]
    h2 = [h_ref[0, t, :] + _dot(m, wo_ref[...]) for t, m in zip(tiles, merged)]
    hn = [_rms(x, g_ref[...]).astype(BF16) for x in h2]
    qx = [_dot(x, wcq_ref[...]).astype(BF16) for x in hn]
    sc = [[_dot_nt(q[:, sl], kx_ref[0, :, sl]) * (hd ** -0.5) for sl in heads] for q in qx]
    outs = []
    for sc_t in sc:
        o_t = []
        for s_h, sl in zip(sc_t, heads):
            e = jnp.exp(s_h - jnp.max(s_h, axis=-1, keepdims=True))
            p = e / jnp.sum(e, axis=-1, keepdims=True)
            o_t.append(_dot(p.astype(BF16), vx_ref[0, :, sl]).astype(BF16))
        outs.append(jnp.concatenate(o_t, axis=1))
    for t, x, o in zip(tiles, h2, outs):
        o_ref[0, t, :] = x + _dot(o, wco_ref[...])


def _mix_cross(h1, y_sb, yc, g_sb, kv, w_attn_out, w_o, g_cross, w_cq, w_co, *, tm, sub):
    b, s, d = h1.shape
    m = kv.shape[1]
    const = lambda bi, j: (0, 0)
    tile = lambda bi, j: (bi, j, 0)
    wspec = pl.BlockSpec((d, d), const, pipeline_mode=pl.Buffered(1))
    return pl.pallas_call(
        functools.partial(_mix_cross_kernel, d=d, sub=sub),
        out_shape=jax.ShapeDtypeStruct((b, s, d), F32),
        grid=(b, s // tm),
        in_specs=[
            pl.BlockSpec((1, tm, d), tile),
            pl.BlockSpec((1, tm, d), tile),
            pl.BlockSpec((1, tm, d), tile),
            pl.BlockSpec((1, tm, d), tile),
            pl.BlockSpec((1, m, d), lambda bi, j: (bi, 0, 0)),
            pl.BlockSpec((1, m, d), lambda bi, j: (bi, 0, 1)),
            wspec, wspec,
            pl.BlockSpec((1, d), const),
            wspec, wspec,
        ],
        out_specs=pl.BlockSpec((1, tm, d), tile),
        compiler_params=pltpu.CompilerParams(
            dimension_semantics=("arbitrary", "arbitrary"), vmem_limit_bytes=V7X_VMEM_LIMIT),
        name="mix_cross",
    )(h1, y_sb, yc, g_sb, kv, kv, w_attn_out, w_o, g_cross, w_cq, w_co)


def kernel(x, mem, g_ffn1, w_ffn1_gu, w_ffn1_down, g_mix, w_in, b_gate, conv_w,
           w_conv_out, w_attn_out, w_o, g_cross, g_mem, w_cq, w_ckv, w_co,
           g_ffn2, w_ffn2_gu, w_ffn2_down, g_final):
    b, s, d = x.shape
    row = lambda v: v.reshape(1, -1).astype(F32)
    wb = lambda w: w.astype(BF16)

    h1 = _ffn(x.reshape(b * s, d), row(g_ffn1), wb(w_ffn1_gu), wb(w_ffn1_down), row(g_final),
              final_norm=False, tm=1024, sub=256, name="ffn1").reshape(b, s, d)
    qkv, yc, g_sb = _mixer_in(h1, row(g_mix), wb(w_in), row(b_gate), conv_w.astype(F32),
                              wb(w_conv_out), tm=512, sub=256)
    y_sb = _sb_attn(qkv, blk=128, heads=SB_HEADS, tiles=2)
    kv = _mem_kv(mem, row(g_mem), wb(w_ckv), tm=1024)
    h3 = _mix_cross(h1, y_sb, yc, g_sb, kv, wb(w_attn_out), wb(w_o), row(g_cross),
                    wb(w_cq), wb(w_co), tm=512, sub=256)
    out = _ffn(h3.reshape(b * s, d), row(g_ffn2), wb(w_ffn2_gu), wb(w_ffn2_down), row(g_final),
               final_norm=True, tm=1024, sub=256, name="ffn2")
    return out.reshape(b, s, d)
```

```python
import functools

import jax
import jax.numpy as jnp
from jax import lax
from jax.experimental import pallas as pl
from jax.experimental.pallas import tpu as pltpu

F32 = jnp.float32
BF16 = jnp.bfloat16

RMS_EPS = 1e-6
SB_HEADS = 8
SB_HEAD_DIM = 128
X_HEADS = 4
CONV_K = 3
LOG2_E = 1.4426950408889634
EXP2_UNDERFLOW = -151.0
DEAD_CARRY = -1e30

V7X_VMEM_LIMIT = 56 * 1024 * 1024
SUBLANES = 8
BF16_SUBLANES = 16


def _rms(x, g):
    ms = jnp.mean(x * x, axis=-1, keepdims=True)
    return (x * lax.rsqrt(ms + RMS_EPS)) * g


def _dot(a, b):
    return jnp.dot(a, b, preferred_element_type=F32)


def _dot_nt(a, b):
    return lax.dot_general(a, b, (((1,), (1,)), ((), ())), preferred_element_type=F32)


def _side_cast_specs(weights, nsteps, index_map):
    in_specs, out_specs, out_shapes = [], [], []
    for w in weights:
        rows, cols = w.shape
        slab = rows // nsteps
        assert slab * nsteps == rows and slab % BF16_SUBLANES == 0, (w.shape, nsteps)
        in_specs.append(pl.BlockSpec((slab, cols), index_map))
        out_specs.append(pl.BlockSpec((slab, cols), index_map))
        out_shapes.append(jax.ShapeDtypeStruct((rows, cols), BF16))
    return in_specs, out_specs, out_shapes


def _side_cast(in_refs, out_refs):
    for src, dst in zip(in_refs, out_refs):
        dst[...] = src[...].astype(BF16)


def _ffn_kernel(x_ref, g_ref, wgu_ref, wd_ref, gf_ref, *rest, d_ff, final_norm, sub):
    nside = (len(rest) - 1) // 2
    o_ref = rest[nside]
    _side_cast(rest[:nside], rest[nside + 1:])
    for r in range(x_ref.shape[0] // sub):
        rows = pl.ds(r * sub, sub)
        x = x_ref[rows, :]
        n = _rms(x, g_ref[...]).astype(BF16)
        gu = _dot(n, wgu_ref[...])
        gate = gu[:, :d_ff]
        up = gu[:, d_ff:]
        a = (gate * jax.nn.sigmoid(gate) * up).astype(BF16)
        y = x + 0.5 * _dot(a, wd_ref[...])
        if final_norm:
            y = _rms(y, gf_ref[...])
        o_ref[rows, :] = y


def _ffn(x2d, g, w_gu, w_down, g_final, *, final_norm, tm, sub, name, side=()):
    t, d = x2d.shape
    d_ff = w_down.shape[0]
    const = lambda i: (0, 0)
    tile = lambda i: (i, 0)
    side_in, side_out, side_shapes = _side_cast_specs(side, t // tm, tile)
    outs = pl.pallas_call(
        functools.partial(_ffn_kernel, d_ff=d_ff, final_norm=final_norm, sub=sub),
        out_shape=[jax.ShapeDtypeStruct((t, d), F32)] + side_shapes,
        grid=(t // tm,),
        in_specs=[
            pl.BlockSpec((tm, d), tile),
            pl.BlockSpec((1, d), const),
            pl.BlockSpec((d, 2 * d_ff), const, pipeline_mode=pl.Buffered(1)),
            pl.BlockSpec((d_ff, d), const, pipeline_mode=pl.Buffered(1)),
            pl.BlockSpec((1, d), const),
        ] + side_in,
        out_specs=[pl.BlockSpec((tm, d), tile)] + side_out,
        compiler_params=pltpu.CompilerParams(
            dimension_semantics=("arbitrary",), vmem_limit_bytes=V7X_VMEM_LIMIT),
        name=name,
    )(x2d, g, w_gu, w_down, g_final, *side)
    return outs[0], outs[1:]


def _mixer_in_kernel(h_ref, g_ref, win_ref, bg_ref, cw_ref, wco_ref,
                     qkv_ref, yc_ref, gsb_ref, carry_ref, *, d, sub):
    j = pl.program_id(1)

    @pl.when(j == 0)
    def _():
        carry_ref[...] = jnp.zeros_like(carry_ref)

    cw = cw_ref[...]
    prev = carry_ref[...]
    row = lax.broadcasted_iota(jnp.int32, (sub, d), 0)
    for r in range(h_ref.shape[1] // sub):
        rows = pl.ds(r * sub, sub)
        u = _rms(h_ref[0, rows, :], g_ref[...]).astype(BF16)
        conv_in = _dot(u, win_ref[:, 0:3 * d])
        qkv_ref[0, rows, :] = _dot(u, win_ref[:, 3 * d:6 * d]).astype(BF16)
        gates = jax.nn.sigmoid(_dot(u, win_ref[:, 6 * d:8 * d]) + bg_ref[...])
        gsb_ref[0, rows, :] = gates[:, d:]

        p = conv_in[:, d:2 * d] * conv_in[:, 2 * d:3 * d]
        p1 = pltpu.roll(p, 1, 0)
        p1 = jnp.where(row == 0, prev[SUBLANES - 1:SUBLANES, :], p1)
        p2 = pltpu.roll(p, 2, 0)
        p2 = jnp.where(row == 0, prev[SUBLANES - 2:SUBLANES - 1, :], p2)
        p2 = jnp.where(row == 1, prev[SUBLANES - 1:SUBLANES, :], p2)
        y_conv = conv_in[:, 0:d] * (cw[0:1, :] * p2 + cw[1:2, :] * p1 + cw[2:3, :] * p)
        prev = p[sub - SUBLANES:, :]
        yc_ref[0, rows, :] = gates[:, :d] * _dot(y_conv.astype(BF16), wco_ref[...])
    carry_ref[...] = prev


def _mixer_in(h1, g, w_in, b_gate, conv_w, w_conv_out, *, tm, sub):
    b, s, d = h1.shape
    const = lambda bi, j: (0, 0)
    tile = lambda bi, j: (bi, j, 0)
    return pl.pallas_call(
        functools.partial(_mixer_in_kernel, d=d, sub=sub),
        out_shape=(
            jax.ShapeDtypeStruct((b, s, 3 * d), BF16),
            jax.ShapeDtypeStruct((b, s, d), F32),
            jax.ShapeDtypeStruct((b, s, d), F32),
        ),
        grid=(b, s // tm),
        in_specs=[
            pl.BlockSpec((1, tm, d), tile),
            pl.BlockSpec((1, d), const),
            pl.BlockSpec((d, 8 * d), const, pipeline_mode=pl.Buffered(1)),
            pl.BlockSpec((1, 2 * d), const),
            pl.BlockSpec((CONV_K, d), const),
            pl.BlockSpec((d, d), const, pipeline_mode=pl.Buffered(1)),
        ],
        out_specs=(
            pl.BlockSpec((1, tm, 3 * d), tile),
            pl.BlockSpec((1, tm, d), tile),
            pl.BlockSpec((1, tm, d), tile),
        ),
        scratch_shapes=[pltpu.VMEM((SUBLANES, d), F32)],
        compiler_params=pltpu.CompilerParams(
            dimension_semantics=("arbitrary", "arbitrary"), vmem_limit_bytes=V7X_VMEM_LIMIT),
        name="mixer_in",
    )(h1, g, w_in, b_gate, conv_w, w_conv_out)


def _sb_attn_kernel(q_ref, k_ref, v_ref, *rest, blk, heads, tiles):
    nside = (len(rest) - 3) // 2
    o_ref = rest[nside]
    carry_ref, acc_ref = rest[-2:]
    _side_cast(rest[:nside], rest[nside + 1:-2])

    s = q_ref.shape[1]
    ngroups = s // (blk * tiles)
    hd = SB_HEAD_DIM
    c2 = hd ** -0.5 * LOG2_E

    row = lax.broadcasted_iota(jnp.int32, (blk, blk), 0)
    col = lax.broadcasted_iota(jnp.int32, (blk, blk), 1)
    causal = col < row
    tri_ones = jnp.concatenate(
        [causal.astype(BF16), jnp.ones((blk, blk), BF16)], axis=1)
    tri_ones2 = jnp.concatenate([tri_ones, tri_ones], axis=0)
    cols = [slice(g * hd, (g + 1) * hd) for g in range(heads)]
    half = blk // 2

    def step(i0, n, diagonal, nrows):
        qoffs, koffs, spent = [], [], []
        for t in range(tiles):
            j = i0 + t - n
            qoffs.append(pl.multiple_of((i0 + t) * blk, blk))
            koffs.append(pl.multiple_of(jnp.maximum(j, 0) * blk, blk))
            spent.append(j <= 0)
        chains = [(t, g) for t in range(tiles) for g in range(heads)]
        mask = causal[:nrows]
        z2s = [_dot_nt(q_ref[0, pl.ds(qoffs[t], nrows), cols[g]],
                       k_ref[0, pl.ds(koffs[t], blk), cols[g]]) * c2 for t, g in chains]
        log_betas, rs = [], []
        for z2 in z2s:
            sp = jnp.log2(1.0 + jnp.exp2(-jnp.abs(z2)))
            log_beta = jnp.minimum(z2, 0.0) - sp
            log_1m = log_beta - z2
            log_betas.append(log_beta)
            if diagonal:
                log_1m = jnp.where(mask, log_1m, 0.0)
            hi = log_1m.astype(BF16)
            lo = (log_1m - hi.astype(F32)).astype(BF16)
            rs.append(_dot(jnp.concatenate([hi, lo], axis=1), tri_ones2))
        m = None
        for c, (t, g) in enumerate(chains):
            after = rs[c][:, :blk]
            carry = rs[c][:, blk:]
            if not diagonal:
                old = carry_ref[c, 0:nrows, :]
                after = after + old
                carry = carry + old
            a = jnp.exp2(log_betas[c] + after)
            if diagonal:
                a = jnp.where(mask, a, 0.0)
            pv = _dot(a.astype(BF16), v_ref[0, pl.ds(koffs[t], blk), cols[g]])
            acc_ref[c, 0:nrows, :] = pv if diagonal else acc_ref[c, 0:nrows, :] + pv
            if t < tiles - 1:
                carry = jnp.where(spent[t], DEAD_CARRY, carry)
            carry_ref[c, 0:nrows, :] = carry
            m = carry if m is None else jnp.maximum(m, carry)
        if nrows == blk:
            return jnp.max(m[:half]), jnp.max(m[half:])
        return jnp.max(m)

    def q_group(grp, _):
        i0 = grp * tiles
        last = i0 + tiles - 1
        top, bottom = step(i0, 0, True, blk)

        def whole_cond(st):
            return jnp.logical_and(st[0] <= last, st[2] >= EXP2_UNDERFLOW)

        def whole_body(st):
            return (st[0] + 1,) + step(i0, st[0], False, blk)

        n, top, _ = lax.while_loop(whole_cond, whole_body, (1, top, bottom))

        def half_cond(st):
            return jnp.logical_and(st[0] <= last, st[1] >= EXP2_UNDERFLOW)

        def half_body(st):
            return st[0] + 1, step(i0, st[0], False, half)

        lax.while_loop(half_cond, half_body, (n, top))
        for t in range(tiles):
            rows = pl.ds(pl.multiple_of((i0 + t) * blk, blk), blk)
            for g in range(heads):
                o_ref[0, rows, cols[g]] = acc_ref[t * heads + g].astype(o_ref.dtype)
        return 0

    lax.fori_loop(0, ngroups, q_group, 0)


def _sb_attn(qkv, *, blk, heads, tiles, side=()):
    b, s, d3 = qkv.shape
    d = d3 // 3
    hd = SB_HEAD_DIM
    w = heads * hd
    ng = d // w
    side_in, side_out, side_shapes = _side_cast_specs(
        side, b * ng, lambda bi, h: (bi * ng + h, 0))
    outs = pl.pallas_call(
        functools.partial(_sb_attn_kernel, blk=blk, heads=heads, tiles=tiles),
        out_shape=[jax.ShapeDtypeStruct((b, s, d), BF16)] + side_shapes,
        grid=(b, ng),
        in_specs=[
            pl.BlockSpec((1, s, w), lambda bi, h: (bi, 0, h)),
            pl.BlockSpec((1, s, w), lambda bi, h: (bi, 0, ng + h)),
            pl.BlockSpec((1, s, w), lambda bi, h: (bi, 0, 2 * ng + h)),
        ] + side_in,
        out_specs=[pl.BlockSpec((1, s, w), lambda bi, h: (bi, 0, h))] + side_out,
        scratch_shapes=[
            pltpu.VMEM((tiles * heads, blk, blk), F32),
            pltpu.VMEM((tiles * heads, blk, hd), F32),
        ],
        compiler_params=pltpu.CompilerParams(
            dimension_semantics=("arbitrary", "arbitrary"), vmem_limit_bytes=V7X_VMEM_LIMIT),
        name="sb_attn",
    )(qkv, qkv, qkv, *side)
    return outs[0], outs[1:]


def _mem_kv_kernel(m_ref, g_ref, w_ref, o_ref):
    mn = _rms(m_ref[...], g_ref[...]).astype(BF16)
    o_ref[...] = _dot(mn, w_ref[...]).astype(BF16)


def _mem_kv(mem, g, w_ckv, *, tm):
    b, m, d = mem.shape
    const = lambda i: (0, 0)
    kv = pl.pallas_call(
        _mem_kv_kernel,
        out_shape=jax.ShapeDtypeStruct((b * m, 2 * d), BF16),
        grid=(b * m // tm,),
        in_specs=[
            pl.BlockSpec((tm, d), lambda i: (i, 0)),
            pl.BlockSpec((1, d), const),
            pl.BlockSpec((d, 2 * d), const, pipeline_mode=pl.Buffered(1)),
        ],
        out_specs=pl.BlockSpec((tm, 2 * d), lambda i: (i, 0)),
        compiler_params=pltpu.CompilerParams(
            dimension_semantics=("arbitrary",), vmem_limit_bytes=V7X_VMEM_LIMIT),
        name="mem_kv",
    )(mem.reshape(b * m, d), g, w_ckv)
    return kv.reshape(b, m, 2 * d)


def _mix_cross_kernel(h_ref, ysb_ref, yc_ref, gsb_ref, kx_ref, vx_ref,
                      wao_ref, wo_ref, g_ref, wcq_ref, wco_ref, o_ref, *, d, sub):
    hd = d // X_HEADS
    tiles = [pl.ds(r * sub, sub) for r in range(h_ref.shape[1] // sub)]
    heads = [slice(hh * hd, (hh + 1) * hd) for hh in range(X_HEADS)]
    att = [_dot(ysb_ref[0, t, :], wao_ref[...]) for t in tiles]
    merged = [(yc_ref[0, t, :] + gsb_ref[0, t, :] * a).astype(BF16) for t, a in zip(tiles, att)]
    h2 = [h_ref[0, t, :] + _dot(m, wo_ref[...]) for t, m in zip(tiles, merged)]
    hn = [_rms(x, g_ref[...]).astype(BF16) for x in h2]
    qx = [_dot(x, wcq_ref[...]).astype(BF16) for x in hn]
    q_all = jnp.concatenate(qx, axis=0)
    sc = [_dot_nt(q_all[:, sl], kx_ref[0, :, sl]) * (hd ** -0.5) for sl in heads]
    o_h = []
    for s_h, sl in zip(sc, heads):
        e = jnp.exp(s_h - jnp.max(s_h, axis=-1, keepdims=True))
        p = e / jnp.sum(e, axis=-1, keepdims=True)
        o_h.append(_dot(p.astype(BF16), vx_ref[0, :, sl]).astype(BF16))
    o_all = jnp.concatenate(o_h, axis=1)
    for r, (t, x) in enumerate(zip(tiles, h2)):
        o_ref[0, t, :] = x + _dot(o_all[r * sub:(r + 1) * sub], wco_ref[...])


def _mix_cross(h1, y_sb, yc, g_sb, kv, w_attn_out, w_o, g_cross, w_cq, w_co, *, tm, sub):
    b, s, d = h1.shape
    m = kv.shape[1]
    const = lambda bi, j: (0, 0)
    tile = lambda bi, j: (bi, j, 0)
    wspec = pl.BlockSpec((d, d), const, pipeline_mode=pl.Buffered(1))
    return pl.pallas_call(
        functools.partial(_mix_cross_kernel, d=d, sub=sub),
        out_shape=jax.ShapeDtypeStruct((b, s, d), F32),
        grid=(b, s // tm),
        in_specs=[
            pl.BlockSpec((1, tm, d), tile),
            pl.BlockSpec((1, tm, d), tile),
            pl.BlockSpec((1, tm, d), tile),
            pl.BlockSpec((1, tm, d), tile),
            pl.BlockSpec((1, m, d), lambda bi, j: (bi, 0, 0)),
            pl.BlockSpec((1, m, d), lambda bi, j: (bi, 0, 1)),
            wspec, wspec,
            pl.BlockSpec((1, d), const),
            wspec, wspec,
        ],
        out_specs=pl.BlockSpec((1, tm, d), tile),
        compiler_params=pltpu.CompilerParams(
            dimension_semantics=("arbitrary", "arbitrary"), vmem_limit_bytes=V7X_VMEM_LIMIT),
        name="mix_cross",
    )(h1, y_sb, yc, g_sb, kv, kv, w_attn_out, w_o, g_cross, w_cq, w_co)


def kernel(x, mem, g_ffn1, w_ffn1_gu, w_ffn1_down, g_mix, w_in, b_gate, conv_w,
           w_conv_out, w_attn_out, w_o, g_cross, g_mem, w_cq, w_ckv, w_co,
           g_ffn2, w_ffn2_gu, w_ffn2_down, g_final):
    b, s, d = x.shape
    row = lambda v: v.reshape(1, -1).astype(F32)
    wb = lambda w: w.astype(BF16)

    h1, (w_in_b, w_conv_out_b, w_attn_out_b, w_o_b, w_cq_b, w_ckv_b, w_co_b) = _ffn(
        x.reshape(b * s, d), row(g_ffn1), wb(w_ffn1_gu), wb(w_ffn1_down), row(g_final),
        final_norm=False, tm=1024, sub=256, name="ffn1",
        side=(w_in, w_conv_out, w_attn_out, w_o, w_cq, w_ckv, w_co))
    h1 = h1.reshape(b, s, d)
    qkv, yc, g_sb = _mixer_in(h1, row(g_mix), w_in_b, row(b_gate), conv_w.astype(F32),
                              w_conv_out_b, tm=512, sub=256)
    y_sb, (w_ffn2_gu_b, w_ffn2_down_b) = _sb_attn(
        qkv, blk=128, heads=SB_HEADS, tiles=2, side=(w_ffn2_gu, w_ffn2_down))
    kv = _mem_kv(mem, row(g_mem), w_ckv_b, tm=1024)
    h3 = _mix_cross(h1, y_sb, yc, g_sb, kv, w_attn_out_b, w_o_b, row(g_cross),
                    w_cq_b, w_co_b, tm=512, sub=256)
    out, _ = _ffn(h3.reshape(b * s, d), row(g_ffn2), w_ffn2_gu_b, w_ffn2_down_b, row(g_final),
                  final_norm=True, tm=1024, sub=256, name="ffn2")
    return out.reshape(b, s, d)
```

```python
import functools

import jax
import jax.numpy as jnp
from jax import lax
from jax.experimental import pallas as pl
from jax.experimental.pallas import tpu as pltpu

F32 = jnp.float32
BF16 = jnp.bfloat16

RMS_EPS = 1e-6
SB_HEADS = 8
SB_HEAD_DIM = 128
X_HEADS = 4
CONV_K = 3
LOG2_E = 1.4426950408889634
EXP2_UNDERFLOW = -151.0
DEAD_CARRY = -1e30

V7X_VMEM_LIMIT = 56 * 1024 * 1024
SUBLANES = 8
BF16_SUBLANES = 16


def _rms(x, g):
    ms = jnp.mean(x * x, axis=-1, keepdims=True)
    return (x * lax.rsqrt(ms + RMS_EPS)) * g


def _dot(a, b):
    return jnp.dot(a, b, preferred_element_type=F32)


def _dot_nt(a, b):
    return lax.dot_general(a, b, (((1,), (1,)), ((), ())), preferred_element_type=F32)


def _side_cast_specs(weights, nsteps, index_map):
    in_specs, out_specs, out_shapes = [], [], []
    for w in weights:
        rows, cols = w.shape
        slab = rows // nsteps
        assert slab * nsteps == rows and slab % BF16_SUBLANES == 0, (w.shape, nsteps)
        in_specs.append(pl.BlockSpec((slab, cols), index_map))
        out_specs.append(pl.BlockSpec((slab, cols), index_map))
        out_shapes.append(jax.ShapeDtypeStruct((rows, cols), BF16))
    return in_specs, out_specs, out_shapes


def _side_cast(in_refs, out_refs):
    for src, dst in zip(in_refs, out_refs):
        dst[...] = src[...].astype(BF16)


def _ffn_kernel(x_ref, g_ref, wgu_ref, wd_ref, gf_ref, *rest, d_ff, final_norm, sub):
    nside = (len(rest) - 1) // 2
    o_ref = rest[nside]
    _side_cast(rest[:nside], rest[nside + 1:])
    for r in range(x_ref.shape[0] // sub):
        rows = pl.ds(r * sub, sub)
        x = x_ref[rows, :]
        n = _rms(x, g_ref[...]).astype(BF16)
        gu = _dot(n, wgu_ref[...])
        gate = gu[:, :d_ff]
        up = gu[:, d_ff:]
        a = (gate * jax.nn.sigmoid(gate) * up).astype(BF16)
        y = x + 0.5 * _dot(a, wd_ref[...])
        if final_norm:
            y = _rms(y, gf_ref[...])
        o_ref[rows, :] = y


def _ffn(x2d, g, w_gu, w_down, g_final, *, final_norm, tm, sub, name, side=()):
    t, d = x2d.shape
    d_ff = w_down.shape[0]
    const = lambda i: (0, 0)
    tile = lambda i: (i, 0)
    side_in, side_out, side_shapes = _side_cast_specs(side, t // tm, tile)
    outs = pl.pallas_call(
        functools.partial(_ffn_kernel, d_ff=d_ff, final_norm=final_norm, sub=sub),
        out_shape=[jax.ShapeDtypeStruct((t, d), F32)] + side_shapes,
        grid=(t // tm,),
        in_specs=[
            pl.BlockSpec((tm, d), tile),
            pl.BlockSpec((1, d), const),
            pl.BlockSpec((d, 2 * d_ff), const, pipeline_mode=pl.Buffered(1)),
            pl.BlockSpec((d_ff, d), const, pipeline_mode=pl.Buffered(1)),
            pl.BlockSpec((1, d), const),
        ] + side_in,
        out_specs=[pl.BlockSpec((tm, d), tile)] + side_out,
        compiler_params=pltpu.CompilerParams(
            dimension_semantics=("arbitrary",), vmem_limit_bytes=V7X_VMEM_LIMIT),
        name=name,
    )(x2d, g, w_gu, w_down, g_final, *side)
    return outs[0], outs[1:]


def _mixer_in_kernel(h_ref, g_ref, win_ref, bg_ref, cw_ref, wco_ref,
                     qkv_ref, yc_ref, gsb_ref, carry_ref, *, d, sub):
    j = pl.program_id(1)

    @pl.when(j == 0)
    def _():
        carry_ref[...] = jnp.zeros_like(carry_ref)

    cw = cw_ref[...]
    prev = carry_ref[...]
    row = lax.broadcasted_iota(jnp.int32, (sub, d), 0)
    for r in range(h_ref.shape[1] // sub):
        rows = pl.ds(r * sub, sub)
        u = _rms(h_ref[0, rows, :], g_ref[...]).astype(BF16)
        conv_in = _dot(u, win_ref[:, 0:3 * d])
        qkv_ref[0, rows, :] = _dot(u, win_ref[:, 3 * d:6 * d]).astype(BF16)
        gates = jax.nn.sigmoid(_dot(u, win_ref[:, 6 * d:8 * d]) + bg_ref[...])
        gsb_ref[0, rows, :] = gates[:, d:]

        p = conv_in[:, d:2 * d] * conv_in[:, 2 * d:3 * d]
        p1 = pltpu.roll(p, 1, 0)
        p1 = jnp.where(row == 0, prev[SUBLANES - 1:SUBLANES, :], p1)
        p2 = pltpu.roll(p, 2, 0)
        p2 = jnp.where(row == 0, prev[SUBLANES - 2:SUBLANES - 1, :], p2)
        p2 = jnp.where(row == 1, prev[SUBLANES - 1:SUBLANES, :], p2)
        y_conv = conv_in[:, 0:d] * (cw[0:1, :] * p2 + cw[1:2, :] * p1 + cw[2:3, :] * p)
        prev = p[sub - SUBLANES:, :]
        yc_ref[0, rows, :] = gates[:, :d] * _dot(y_conv.astype(BF16), wco_ref[...])
    carry_ref[...] = prev


def _mixer_in(h1, g, w_in, b_gate, conv_w, w_conv_out, *, tm, sub):
    b, s, d = h1.shape
    const = lambda bi, j: (0, 0)
    tile = lambda bi, j: (bi, j, 0)
    return pl.pallas_call(
        functools.partial(_mixer_in_kernel, d=d, sub=sub),
        out_shape=(
            jax.ShapeDtypeStruct((b, s, 3 * d), BF16),
            jax.ShapeDtypeStruct((b, s, d), F32),
            jax.ShapeDtypeStruct((b, s, d), F32),
        ),
        grid=(b, s // tm),
        in_specs=[
            pl.BlockSpec((1, tm, d), tile),
            pl.BlockSpec((1, d), const),
            pl.BlockSpec((d, 8 * d), const, pipeline_mode=pl.Buffered(1)),
            pl.BlockSpec((1, 2 * d), const),
            pl.BlockSpec((CONV_K, d), const),
            pl.BlockSpec((d, d), const, pipeline_mode=pl.Buffered(1)),
        ],
        out_specs=(
            pl.BlockSpec((1, tm, 3 * d), tile),
            pl.BlockSpec((1, tm, d), tile),
            pl.BlockSpec((1, tm, d), tile),
        ),
        scratch_shapes=[pltpu.VMEM((SUBLANES, d), F32)],
        compiler_params=pltpu.CompilerParams(
            dimension_semantics=("arbitrary", "arbitrary"), vmem_limit_bytes=V7X_VMEM_LIMIT),
        name="mixer_in",
    )(h1, g, w_in, b_gate, conv_w, w_conv_out)


def _sb_attn_kernel(q_ref, k_ref, v_ref, *rest, blk, heads, tiles):
    nside = (len(rest) - 3) // 2
    o_ref = rest[nside]
    carry_ref, acc_ref = rest[-2:]
    _side_cast(rest[:nside], rest[nside + 1:-2])

    s = q_ref.shape[1]
    ngroups = s // (blk * tiles)
    hd = SB_HEAD_DIM
    c2 = hd ** -0.5 * LOG2_E

    row = lax.broadcasted_iota(jnp.int32, (blk, blk), 0)
    col = lax.broadcasted_iota(jnp.int32, (blk, blk), 1)
    causal = col < row
    tri_ones = jnp.concatenate(
        [causal.astype(BF16), jnp.ones((blk, blk), BF16)], axis=1)
    tri_ones2 = jnp.concatenate([tri_ones, tri_ones], axis=0)
    cols = [slice(g * hd, (g + 1) * hd) for g in range(heads)]
    half = blk // 2

    def step(i0, blocks, nrows):
        chains = [(t, g) for t in range(tiles) for g in range(heads)]
        mask = causal[:nrows]
        qoffs = [pl.multiple_of((i0 + t) * blk, blk) for t in range(tiles)]
        koffs, spent = [], []
        for n, _ in blocks:
            js = [i0 + t - n for t in range(tiles)]
            koffs.append([pl.multiple_of(jnp.maximum(j, 0) * blk, blk) for j in js])
            spent.append([j <= 0 for j in js])
        z2s = [[_dot_nt(q_ref[0, pl.ds(qoffs[t], nrows), cols[g]],
                        k_ref[0, pl.ds(koff[t], blk), cols[g]]) * c2 for t, g in chains]
               for koff in koffs]
        log_betas, rs = [], []
        for (_, diagonal), z2_blk in zip(blocks, z2s):
            lb_blk, r_blk = [], []
            for z2 in z2_blk:
                sp = jnp.log2(1.0 + jnp.exp2(-jnp.abs(z2)))
                log_beta = jnp.minimum(z2, 0.0) - sp
                log_1m = log_beta - z2
                lb_blk.append(log_beta)
                if diagonal:
                    log_1m = jnp.where(mask, log_1m, 0.0)
                hi = log_1m.astype(BF16)
                lo = (log_1m - hi.astype(F32)).astype(BF16)
                r_blk.append(_dot(jnp.concatenate([hi, lo], axis=1), tri_ones2))
            log_betas.append(lb_blk)
            rs.append(r_blk)
        carries = [None] * len(chains)
        accs = [None] * len(chains)
        for b, (_, diagonal) in enumerate(blocks):
            for c, (t, g) in enumerate(chains):
                after = rs[b][c][:, :blk]
                carry = rs[b][c][:, blk:]
                if not diagonal:
                    old = carry_ref[c, 0:nrows, :] if b == 0 else carries[c]
                    after = after + old
                    carry = carry + old
                a = jnp.exp2(log_betas[b][c] + after)
                if diagonal:
                    a = jnp.where(mask, a, 0.0)
                pv = _dot(a.astype(BF16), v_ref[0, pl.ds(koffs[b][t], blk), cols[g]])
                if b > 0:
                    accs[c] = accs[c] + pv
                else:
                    accs[c] = pv if diagonal else acc_ref[c, 0:nrows, :] + pv
                if t < tiles - 1:
                    carry = jnp.where(spent[b][t], DEAD_CARRY, carry)
                carries[c] = carry
        m = None
        for c in range(len(chains)):
            acc_ref[c, 0:nrows, :] = accs[c]
            carry_ref[c, 0:nrows, :] = carries[c]
            m = carries[c] if m is None else jnp.maximum(m, carries[c])
        if nrows == blk:
            return jnp.max(m[:half]), jnp.max(m[half:])
        return jnp.max(m)

    def q_group(grp, _):
        i0 = grp * tiles
        last = i0 + tiles - 1
        top, bottom = step(i0, ((0, True), (1, False)), blk)

        def whole_cond(st):
            return jnp.logical_and(st[0] <= last, st[2] >= EXP2_UNDERFLOW)

        def whole_body(st):
            return (st[0] + 1,) + step(i0, ((st[0], False),), blk)

        n, top, _ = lax.while_loop(whole_cond, whole_body, (2, top, bottom))

        def half_cond(st):
            return jnp.logical_and(st[0] <= last, st[1] >= EXP2_UNDERFLOW)

        def half_body(st):
            return st[0] + 1, step(i0, ((st[0], False),), half)

        lax.while_loop(half_cond, half_body, (n, top))
        for t in range(tiles):
            rows = pl.ds(pl.multiple_of((i0 + t) * blk, blk), blk)
            for g in range(heads):
                o_ref[0, rows, cols[g]] = acc_ref[t * heads + g].astype(o_ref.dtype)
        return 0

    lax.fori_loop(0, ngroups, q_group, 0)


def _sb_attn(qkv, *, blk, heads, tiles, side=()):
    b, s, d3 = qkv.shape
    d = d3 // 3
    hd = SB_HEAD_DIM
    w = heads * hd
    ng = d // w
    side_in, side_out, side_shapes = _side_cast_specs(
        side, b * ng, lambda bi, h: (bi * ng + h, 0))
    outs = pl.pallas_call(
        functools.partial(_sb_attn_kernel, blk=blk, heads=heads, tiles=tiles),
        out_shape=[jax.ShapeDtypeStruct((b, s, d), BF16)] + side_shapes,
        grid=(b, ng),
        in_specs=[
            pl.BlockSpec((1, s, w), lambda bi, h: (bi, 0, h)),
            pl.BlockSpec((1, s, w), lambda bi, h: (bi, 0, ng + h)),
            pl.BlockSpec((1, s, w), lambda bi, h: (bi, 0, 2 * ng + h)),
        ] + side_in,
        out_specs=[pl.BlockSpec((1, s, w), lambda bi, h: (bi, 0, h))] + side_out,
        scratch_shapes=[
            pltpu.VMEM((tiles * heads, blk, blk), F32),
            pltpu.VMEM((tiles * heads, blk, hd), F32),
        ],
        compiler_params=pltpu.CompilerParams(
            dimension_semantics=("arbitrary", "arbitrary"), vmem_limit_bytes=V7X_VMEM_LIMIT),
        name="sb_attn",
    )(qkv, qkv, qkv, *side)
    return outs[0], outs[1:]


def _mem_kv_kernel(m_ref, g_ref, w_ref, *rest):
    nside = (len(rest) - 1) // 2
    o_ref = rest[nside]
    _side_cast(rest[:nside], rest[nside + 1:])
    mn = _rms(m_ref[...], g_ref[...]).astype(BF16)
    o_ref[...] = _dot(mn, w_ref[...].astype(BF16)).astype(BF16)


def _mem_kv(mem, g, w_ckv, *, tm, side=()):
    b, m, d = mem.shape
    const = lambda i: (0, 0)
    tile = lambda i: (i, 0)
    side_in, side_out, side_shapes = _side_cast_specs(side, b * m // tm, tile)
    outs = pl.pallas_call(
        _mem_kv_kernel,
        out_shape=[jax.ShapeDtypeStruct((b * m, 2 * d), BF16)] + side_shapes,
        grid=(b * m // tm,),
        in_specs=[
            pl.BlockSpec((tm, d), tile),
            pl.BlockSpec((1, d), const),
            pl.BlockSpec((d, 2 * d), const, pipeline_mode=pl.Buffered(1)),
        ] + side_in,
        out_specs=[pl.BlockSpec((tm, 2 * d), tile)] + side_out,
        compiler_params=pltpu.CompilerParams(
            dimension_semantics=("arbitrary",), vmem_limit_bytes=V7X_VMEM_LIMIT),
        name="mem_kv",
    )(mem.reshape(b * m, d), g, w_ckv, *side)
    return outs[0].reshape(b, m, 2 * d), outs[1:]


def _mix_cross_kernel(h_ref, ysb_ref, yc_ref, gsb_ref, kx_ref, vx_ref,
                      wao_ref, wo_ref, g_ref, wcq_ref, wco_ref, o_ref, *, d, sub):
    hd = d // X_HEADS
    tiles = [pl.ds(r * sub, sub) for r in range(h_ref.shape[1] // sub)]
    heads = [slice(hh * hd, (hh + 1) * hd) for hh in range(X_HEADS)]
    att = [_dot(ysb_ref[0, t, :], wao_ref[...]) for t in tiles]
    merged = [(yc_ref[0, t, :] + gsb_ref[0, t, :] * a).astype(BF16) for t, a in zip(tiles, att)]
    h2 = [h_ref[0, t, :] + _dot(m, wo_ref[...]) for t, m in zip(tiles, merged)]
    hn = [_rms(x, g_ref[...]).astype(BF16) for x in h2]
    qx = [_dot(x, wcq_ref[...]).astype(BF16) for x in hn]
    q_all = jnp.concatenate(qx, axis=0)
    sc = [_dot_nt(q_all[:, sl], kx_ref[0, :, sl]) * (hd ** -0.5) for sl in heads]
    o_h = []
    for s_h, sl in zip(sc, heads):
        e = jnp.exp(s_h - jnp.max(s_h, axis=-1, keepdims=True))
        p = e / jnp.sum(e, axis=-1, keepdims=True)
        o_h.append(_dot(p.astype(BF16), vx_ref[0, :, sl]).astype(BF16))
    o_all = jnp.concatenate(o_h, axis=1)
    for r, (t, x) in enumerate(zip(tiles, h2)):
        o_ref[0, t, :] = x + _dot(o_all[r * sub:(r + 1) * sub], wco_ref[...])


def _mix_cross(h1, y_sb, yc, g_sb, kv, w_attn_out, w_o, g_cross, w_cq, w_co, *, tm, sub):
    b, s, d = h1.shape
    m = kv.shape[1]
    const = lambda bi, j: (0, 0)
    tile = lambda bi, j: (bi, j, 0)
    wspec = pl.BlockSpec((d, d), const, pipeline_mode=pl.Buffered(1))
    return pl.pallas_call(
        functools.partial(_mix_cross_kernel, d=d, sub=sub),
        out_shape=jax.ShapeDtypeStruct((b, s, d), F32),
        grid=(b, s // tm),
        in_specs=[
            pl.BlockSpec((1, tm, d), tile),
            pl.BlockSpec((1, tm, d), tile),
            pl.BlockSpec((1, tm, d), tile),
            pl.BlockSpec((1, tm, d), tile),
            pl.BlockSpec((1, m, d), lambda bi, j: (bi, 0, 0)),
            pl.BlockSpec((1, m, d), lambda bi, j: (bi, 0, 1)),
            wspec, wspec,
            pl.BlockSpec((1, d), const),
            wspec, wspec,
        ],
        out_specs=pl.BlockSpec((1, tm, d), tile),
        compiler_params=pltpu.CompilerParams(
            dimension_semantics=("arbitrary", "arbitrary"), vmem_limit_bytes=V7X_VMEM_LIMIT),
        name="mix_cross",
    )(h1, y_sb, yc, g_sb, kv, kv, w_attn_out, w_o, g_cross, w_cq, w_co)


def kernel(x, mem, g_ffn1, w_ffn1_gu, w_ffn1_down, g_mix, w_in, b_gate, conv_w,
           w_conv_out, w_attn_out, w_o, g_cross, g_mem, w_cq, w_ckv, w_co,
           g_ffn2, w_ffn2_gu, w_ffn2_down, g_final):
    b, s, d = x.shape
    row = lambda v: v.reshape(1, -1).astype(F32)

    kv, (w_ffn1_gu_b, w_ffn1_down_b) = _mem_kv(
        mem, row(g_mem), w_ckv, tm=256, side=(w_ffn1_gu, w_ffn1_down))
    h1, (w_in_b, w_conv_out_b, w_attn_out_b, w_o_b, w_cq_b, w_co_b) = _ffn(
        x.reshape(b * s, d), row(g_ffn1), w_ffn1_gu_b, w_ffn1_down_b, row(g_final),
        final_norm=False, tm=1024, sub=256, name="ffn1",
        side=(w_in, w_conv_out, w_attn_out, w_o, w_cq, w_co))
    h1 = h1.reshape(b, s, d)
    qkv, yc, g_sb = _mixer_in(h1, row(g_mix), w_in_b, row(b_gate), conv_w.astype(F32),
                              w_conv_out_b, tm=512, sub=256)
    y_sb, (w_ffn2_gu_b, w_ffn2_down_b) = _sb_attn(
        qkv, blk=128, heads=SB_HEADS, tiles=2, side=(w_ffn2_gu, w_ffn2_down))
    h3 = _mix_cross(h1, y_sb, yc, g_sb, kv, w_attn_out_b, w_o_b, row(g_cross),
                    w_cq_b, w_co_b, tm=512, sub=256)
    out, _ = _ffn(h3.reshape(b * s, d), row(g_ffn2), w_ffn2_gu_b, w_ffn2_down_b, row(g_final),
                  final_norm=True, tm=1024, sub=256, name="ffn2")
    return out.reshape(b, s, d)
```

```python
import functools

import jax
import jax.numpy as jnp
from jax import lax
from jax.experimental import pallas as pl
from jax.experimental.pallas import tpu as pltpu

F32 = jnp.float32
BF16 = jnp.bfloat16

RMS_EPS = 1e-6
SB_HEADS = 8
SB_HEAD_DIM = 128
X_HEADS = 4
CONV_K = 3
LOG2_E = 1.4426950408889634
EXP2_UNDERFLOW = -151.0
DEAD_CARRY = -1e30

V7X_VMEM_LIMIT = 60 * 1024 * 1024
SUBLANES = 8
BF16_SUBLANES = 16


def _rms(x, g):
    ms = jnp.mean(x * x, axis=-1, keepdims=True)
    return (x * lax.rsqrt(ms + RMS_EPS)) * g


def _dot(a, b):
    return jnp.dot(a, b, preferred_element_type=F32)


def _dot_nt(a, b):
    return lax.dot_general(a, b, (((1,), (1,)), ((), ())), preferred_element_type=F32)


def _side_cast_specs(weights, nsteps, index_map):
    in_specs, out_specs, out_shapes = [], [], []
    for w in weights:
        rows, cols = w.shape
        slab = rows // nsteps
        assert slab * nsteps == rows and slab % BF16_SUBLANES == 0, (w.shape, nsteps)
        in_specs.append(pl.BlockSpec((slab, cols), index_map))
        out_specs.append(pl.BlockSpec((slab, cols), index_map))
        out_shapes.append(jax.ShapeDtypeStruct((rows, cols), BF16))
    return in_specs, out_specs, out_shapes


def _side_cast(in_refs, out_refs):
    for src, dst in zip(in_refs, out_refs):
        dst[...] = src[...].astype(BF16)


def _ffn_kernel(x_ref, g_ref, wgu_ref, wd_ref, gf_ref, *rest, d_ff, final_norm, sub):
    nside = (len(rest) - 1) // 2
    o_ref = rest[nside]
    _side_cast(rest[:nside], rest[nside + 1:])
    for r in range(x_ref.shape[0] // sub):
        rows = pl.ds(r * sub, sub)
        x = x_ref[rows, :]
        n = _rms(x, g_ref[...]).astype(BF16)
        gu = _dot(n, wgu_ref[...])
        gate = gu[:, :d_ff]
        up = gu[:, d_ff:]
        a = (gate * jax.nn.sigmoid(gate) * up).astype(BF16)
        y = x + 0.5 * _dot(a, wd_ref[...])
        if final_norm:
            y = _rms(y, gf_ref[...])
        o_ref[rows, :] = y


def _ffn(x2d, g, w_gu, w_down, g_final, *, final_norm, tm, sub, name, side=()):
    t, d = x2d.shape
    d_ff = w_down.shape[0]
    const = lambda i: (0, 0)
    tile = lambda i: (i, 0)
    side_in, side_out, side_shapes = _side_cast_specs(side, t // tm, tile)
    outs = pl.pallas_call(
        functools.partial(_ffn_kernel, d_ff=d_ff, final_norm=final_norm, sub=sub),
        out_shape=[jax.ShapeDtypeStruct((t, d), F32)] + side_shapes,
        grid=(t // tm,),
        in_specs=[
            pl.BlockSpec((tm, d), tile),
            pl.BlockSpec((1, d), const),
            pl.BlockSpec((d, 2 * d_ff), const, pipeline_mode=pl.Buffered(1)),
            pl.BlockSpec((d_ff, d), const, pipeline_mode=pl.Buffered(1)),
            pl.BlockSpec((1, d), const),
        ] + side_in,
        out_specs=[pl.BlockSpec((tm, d), tile)] + side_out,
        compiler_params=pltpu.CompilerParams(
            dimension_semantics=("arbitrary",), vmem_limit_bytes=V7X_VMEM_LIMIT),
        name=name,
    )(x2d, g, w_gu, w_down, g_final, *side)
    return outs[0], outs[1:]


def _mixer_in_kernel(h_ref, g_ref, win_ref, bg_ref, cw_ref, wco_ref,
                     qkv_ref, yc_ref, gsb_ref, carry_ref, *, d, sub):
    j = pl.program_id(1)

    @pl.when(j == 0)
    def _():
        carry_ref[...] = jnp.zeros_like(carry_ref)

    cw = cw_ref[...]
    prev = carry_ref[...]
    row = lax.broadcasted_iota(jnp.int32, (sub, d), 0)
    for r in range(h_ref.shape[1] // sub):
        rows = pl.ds(r * sub, sub)
        u = _rms(h_ref[0, rows, :], g_ref[...]).astype(BF16)
        conv_in = _dot(u, win_ref[:, 0:3 * d])
        qkv_ref[0, rows, :] = _dot(u, win_ref[:, 3 * d:6 * d]).astype(BF16)
        gates = jax.nn.sigmoid(_dot(u, win_ref[:, 6 * d:8 * d]) + bg_ref[...])
        gsb_ref[0, rows, :] = gates[:, d:]

        p = conv_in[:, d:2 * d] * conv_in[:, 2 * d:3 * d]
        p1 = pltpu.roll(p, 1, 0)
        p1 = jnp.where(row == 0, prev[SUBLANES - 1:SUBLANES, :], p1)
        p2 = pltpu.roll(p, 2, 0)
        p2 = jnp.where(row == 0, prev[SUBLANES - 2:SUBLANES - 1, :], p2)
        p2 = jnp.where(row == 1, prev[SUBLANES - 1:SUBLANES, :], p2)
        y_conv = conv_in[:, 0:d] * (cw[0:1, :] * p2 + cw[1:2, :] * p1 + cw[2:3, :] * p)
        prev = p[sub - SUBLANES:, :]
        yc_ref[0, rows, :] = gates[:, :d] * _dot(y_conv.astype(BF16), wco_ref[...])
    carry_ref[...] = prev


def _mixer_in(h1, g, w_in, b_gate, conv_w, w_conv_out, *, tm, sub):
    b, s, d = h1.shape
    const = lambda bi, j: (0, 0)
    tile = lambda bi, j: (bi, j, 0)
    return pl.pallas_call(
        functools.partial(_mixer_in_kernel, d=d, sub=sub),
        out_shape=(
            jax.ShapeDtypeStruct((b, s, 3 * d), BF16),
            jax.ShapeDtypeStruct((b, s, d), F32),
            jax.ShapeDtypeStruct((b, s, d), F32),
        ),
        grid=(b, s // tm),
        in_specs=[
            pl.BlockSpec((1, tm, d), tile),
            pl.BlockSpec((1, d), const),
            pl.BlockSpec((d, 8 * d), const, pipeline_mode=pl.Buffered(1)),
            pl.BlockSpec((1, 2 * d), const),
            pl.BlockSpec((CONV_K, d), const),
            pl.BlockSpec((d, d), const, pipeline_mode=pl.Buffered(1)),
        ],
        out_specs=(
            pl.BlockSpec((1, tm, 3 * d), tile),
            pl.BlockSpec((1, tm, d), tile),
            pl.BlockSpec((1, tm, d), tile),
        ),
        scratch_shapes=[pltpu.VMEM((SUBLANES, d), F32)],
        compiler_params=pltpu.CompilerParams(
            dimension_semantics=("arbitrary", "arbitrary"), vmem_limit_bytes=V7X_VMEM_LIMIT),
        name="mixer_in",
    )(h1, g, w_in, b_gate, conv_w, w_conv_out)


def _sb_attn_kernel(q_ref, k_ref, v_ref, *rest, blk, heads, tiles):
    nside = (len(rest) - 3) // 2
    o_ref = rest[nside]
    carry_ref, acc_ref = rest[-2:]
    _side_cast(rest[:nside], rest[nside + 1:-2])

    s = q_ref.shape[1]
    ngroups = s // (blk * tiles)
    hd = SB_HEAD_DIM
    c2 = hd ** -0.5 * LOG2_E

    row = lax.broadcasted_iota(jnp.int32, (blk, blk), 0)
    col = lax.broadcasted_iota(jnp.int32, (blk, blk), 1)
    causal = col < row
    tri_ones = jnp.concatenate(
        [causal.astype(BF16), jnp.ones((blk, blk), BF16)], axis=1)
    tri_ones2 = jnp.concatenate([tri_ones, tri_ones], axis=0)
    cols = [slice(g * hd, (g + 1) * hd) for g in range(heads)]
    half = blk // 2

    def step(i0, blocks, nrows):
        chains = [(t, g) for t in range(tiles) for g in range(heads)]
        mask = causal[:nrows]
        qoffs = [pl.multiple_of((i0 + t) * blk, blk) for t in range(tiles)]
        koffs, spent = [], []
        for n, _ in blocks:
            js = [i0 + t - n for t in range(tiles)]
            koffs.append([pl.multiple_of(jnp.maximum(j, 0) * blk, blk) for j in js])
            spent.append([j <= 0 for j in js])
        z2s = [[_dot_nt(q_ref[0, pl.ds(qoffs[t], nrows), cols[g]],
                        k_ref[0, pl.ds(koff[t], blk), cols[g]]) * c2 for t, g in chains]
               for koff in koffs]
        log_betas, rs = [], []
        for (_, diagonal), z2_blk in zip(blocks, z2s):
            lb_blk, r_blk = [], []
            for z2 in z2_blk:
                sp = jnp.log2(1.0 + jnp.exp2(-jnp.abs(z2)))
                log_beta = jnp.minimum(z2, 0.0) - sp
                log_1m = log_beta - z2
                lb_blk.append(log_beta)
                if diagonal:
                    log_1m = jnp.where(mask, log_1m, 0.0)
                hi = log_1m.astype(BF16)
                lo = (log_1m - hi.astype(F32)).astype(BF16)
                r_blk.append(_dot(jnp.concatenate([hi, lo], axis=1), tri_ones2))
            log_betas.append(lb_blk)
            rs.append(r_blk)
        carries = [None] * len(chains)
        accs = [None] * len(chains)
        for b, (_, diagonal) in enumerate(blocks):
            for c, (t, g) in enumerate(chains):
                after = rs[b][c][:, :blk]
                carry = rs[b][c][:, blk:]
                if not diagonal:
                    old = carry_ref[c, 0:nrows, :] if b == 0 else carries[c]
                    after = after + old
                    carry = carry + old
                a = jnp.exp2(log_betas[b][c] + after)
                if diagonal:
                    a = jnp.where(mask, a, 0.0)
                pv = _dot(a.astype(BF16), v_ref[0, pl.ds(koffs[b][t], blk), cols[g]])
                if b > 0:
                    accs[c] = accs[c] + pv
                else:
                    accs[c] = pv if diagonal else acc_ref[c, 0:nrows, :] + pv
                if t < tiles - 1:
                    carry = jnp.where(spent[b][t], DEAD_CARRY, carry)
                carries[c] = carry
        m = None
        for c in range(len(chains)):
            acc_ref[c, 0:nrows, :] = accs[c]
            carry_ref[c, 0:nrows, :] = carries[c]
            m = carries[c] if m is None else jnp.maximum(m, carries[c])
        if nrows == blk:
            return jnp.max(m[:half]), jnp.max(m[half:])
        return jnp.max(m)

    def q_group(grp, _):
        i0 = grp * tiles
        last = i0 + tiles - 1
        top, bottom = step(i0, ((0, True), (1, False)), blk)

        def whole_cond(st):
            return jnp.logical_and(st[0] <= last, st[2] >= EXP2_UNDERFLOW)

        def whole_body(st):
            return (st[0] + 1,) + step(i0, ((st[0], False),), blk)

        n, top, _ = lax.while_loop(whole_cond, whole_body, (2, top, bottom))

        def half_cond(st):
            return jnp.logical_and(st[0] <= last, st[1] >= EXP2_UNDERFLOW)

        def half_body(st):
            return st[0] + 1, step(i0, ((st[0], False),), half)

        lax.while_loop(half_cond, half_body, (n, top))
        for t in range(tiles):
            rows = pl.ds(pl.multiple_of((i0 + t) * blk, blk), blk)
            for g in range(heads):
                o_ref[0, rows, cols[g]] = acc_ref[t * heads + g].astype(o_ref.dtype)
        return 0

    lax.fori_loop(0, ngroups, q_group, 0)


def _sb_attn(qkv, *, blk, heads, tiles, side=()):
    b, s, d3 = qkv.shape
    d = d3 // 3
    hd = SB_HEAD_DIM
    w = heads * hd
    ng = d // w
    side_in, side_out, side_shapes = _side_cast_specs(
        side, b * ng, lambda bi, h: (bi * ng + h, 0))
    outs = pl.pallas_call(
        functools.partial(_sb_attn_kernel, blk=blk, heads=heads, tiles=tiles),
        out_shape=[jax.ShapeDtypeStruct((b, s, d), BF16)] + side_shapes,
        grid=(b, ng),
        in_specs=[
            pl.BlockSpec((1, s, w), lambda bi, h: (bi, 0, h)),
            pl.BlockSpec((1, s, w), lambda bi, h: (bi, 0, ng + h)),
            pl.BlockSpec((1, s, w), lambda bi, h: (bi, 0, 2 * ng + h)),
        ] + side_in,
        out_specs=[pl.BlockSpec((1, s, w), lambda bi, h: (bi, 0, h))] + side_out,
        scratch_shapes=[
            pltpu.VMEM((tiles * heads, blk, blk), F32),
            pltpu.VMEM((tiles * heads, blk, hd), F32),
        ],
        compiler_params=pltpu.CompilerParams(
            dimension_semantics=("arbitrary", "arbitrary"), vmem_limit_bytes=V7X_VMEM_LIMIT),
        name="sb_attn",
    )(qkv, qkv, qkv, *side)
    return outs[0], outs[1:]


def _mem_kv_kernel(m_ref, g_ref, w_ref, *rest):
    nside = (len(rest) - 1) // 2
    o_ref = rest[nside]
    _side_cast(rest[:nside], rest[nside + 1:])
    mn = _rms(m_ref[...], g_ref[...]).astype(BF16)
    o_ref[...] = _dot(mn, w_ref[...].astype(BF16)).astype(BF16)


def _mem_kv(mem, g, w_ckv, *, tm, side=()):
    b, m, d = mem.shape
    const = lambda i: (0, 0)
    tile = lambda i: (i, 0)
    side_in, side_out, side_shapes = _side_cast_specs(side, b * m // tm, tile)
    outs = pl.pallas_call(
        _mem_kv_kernel,
        out_shape=[jax.ShapeDtypeStruct((b * m, 2 * d), BF16)] + side_shapes,
        grid=(b * m // tm,),
        in_specs=[
            pl.BlockSpec((tm, d), tile),
            pl.BlockSpec((1, d), const),
            pl.BlockSpec((d, 2 * d), const, pipeline_mode=pl.Buffered(1)),
        ] + side_in,
        out_specs=[pl.BlockSpec((tm, 2 * d), tile)] + side_out,
        compiler_params=pltpu.CompilerParams(
            dimension_semantics=("arbitrary",), vmem_limit_bytes=V7X_VMEM_LIMIT),
        name="mem_kv",
    )(mem.reshape(b * m, d), g, w_ckv, *side)
    return outs[0].reshape(b, m, 2 * d), outs[1:]


def _mix_cross_kernel(h_ref, ysb_ref, yc_ref, gsb_ref, kx_ref, vx_ref,
                      wao_ref, wo_ref, g_ref, wcq_ref, wco_ref, o_ref, *, d, sub):
    hd = d // X_HEADS
    tiles = [pl.ds(r * sub, sub) for r in range(h_ref.shape[1] // sub)]
    heads = [slice(hh * hd, (hh + 1) * hd) for hh in range(X_HEADS)]
    att = [_dot(ysb_ref[0, t, :], wao_ref[...]) for t in tiles]
    merged = [(yc_ref[0, t, :] + gsb_ref[0, t, :] * a).astype(BF16) for t, a in zip(tiles, att)]
    h2 = [h_ref[0, t, :] + _dot(m, wo_ref[...]) for t, m in zip(tiles, merged)]
    hn = [_rms(x, g_ref[...]).astype(BF16) for x in h2]
    qx = [_dot(x, wcq_ref[...]).astype(BF16) for x in hn]
    q_all = jnp.concatenate(qx, axis=0)
    sc = [_dot_nt(q_all[:, sl], kx_ref[0, :, sl]) * (hd ** -0.5) for sl in heads]
    o_h = []
    for s_h, sl in zip(sc, heads):
        e = jnp.exp(s_h - jnp.max(s_h, axis=-1, keepdims=True))
        p = e / jnp.sum(e, axis=-1, keepdims=True)
        o_h.append(_dot(p.astype(BF16), vx_ref[0, :, sl]).astype(BF16))
    o_all = jnp.concatenate(o_h, axis=1)
    for r, (t, x) in enumerate(zip(tiles, h2)):
        o_ref[0, t, :] = x + _dot(o_all[r * sub:(r + 1) * sub], wco_ref[...])


def _mix_cross(h1, y_sb, yc, g_sb, kv, w_attn_out, w_o, g_cross, w_cq, w_co, *, tm, sub):
    b, s, d = h1.shape
    m = kv.shape[1]
    const = lambda bi, j: (0, 0)
    tile = lambda bi, j: (bi, j, 0)
    wspec = pl.BlockSpec((d, d), const, pipeline_mode=pl.Buffered(1))
    return pl.pallas_call(
        functools.partial(_mix_cross_kernel, d=d, sub=sub),
        out_shape=jax.ShapeDtypeStruct((b, s, d), F32),
        grid=(b, s // tm),
        in_specs=[
            pl.BlockSpec((1, tm, d), tile),
            pl.BlockSpec((1, tm, d), tile),
            pl.BlockSpec((1, tm, d), tile),
            pl.BlockSpec((1, tm, d), tile),
            pl.BlockSpec((1, m, d), lambda bi, j: (bi, 0, 0)),
            pl.BlockSpec((1, m, d), lambda bi, j: (bi, 0, 1)),
            wspec, wspec,
            pl.BlockSpec((1, d), const),
            wspec, wspec,
        ],
        out_specs=pl.BlockSpec((1, tm, d), tile),
        compiler_params=pltpu.CompilerParams(
            dimension_semantics=("arbitrary", "arbitrary"), vmem_limit_bytes=V7X_VMEM_LIMIT),
        name="mix_cross",
    )(h1, y_sb, yc, g_sb, kv, kv, w_attn_out, w_o, g_cross, w_cq, w_co)


def kernel(x, mem, g_ffn1, w_ffn1_gu, w_ffn1_down, g_mix, w_in, b_gate, conv_w,
           w_conv_out, w_attn_out, w_o, g_cross, g_mem, w_cq, w_ckv, w_co,
           g_ffn2, w_ffn2_gu, w_ffn2_down, g_final):
    b, s, d = x.shape
    row = lambda v: v.reshape(1, -1).astype(F32)

    kv, (w_ffn1_gu_b, w_ffn1_down_b) = _mem_kv(
        mem, row(g_mem), w_ckv, tm=256, side=(w_ffn1_gu, w_ffn1_down))
    h1, (w_in_b, w_conv_out_b, w_attn_out_b, w_o_b, w_cq_b, w_co_b) = _ffn(
        x.reshape(b * s, d), row(g_ffn1), w_ffn1_gu_b, w_ffn1_down_b, row(g_final),
        final_norm=False, tm=1024, sub=256, name="ffn1",
        side=(w_in, w_conv_out, w_attn_out, w_o, w_cq, w_co))
    h1 = h1.reshape(b, s, d)
    qkv, yc, g_sb = _mixer_in(h1, row(g_mix), w_in_b, row(b_gate), conv_w.astype(F32),
                              w_conv_out_b, tm=1024, sub=256)
    y_sb, (w_ffn2_gu_b, w_ffn2_down_b) = _sb_attn(
        qkv, blk=128, heads=SB_HEADS, tiles=2, side=(w_ffn2_gu, w_ffn2_down))
    h3 = _mix_cross(h1, y_sb, yc, g_sb, kv, w_attn_out_b, w_o_b, row(g_cross),
                    w_cq_b, w_co_b, tm=1024, sub=256)
    out, _ = _ffn(h3.reshape(b * s, d), row(g_ffn2), w_ffn2_gu_b, w_ffn2_down_b, row(g_final),
                  final_norm=True, tm=1024, sub=256, name="ffn2")
    return out.reshape(b, s, d)
```

```python
import functools

import jax
import jax.numpy as jnp
from jax import lax
from jax.experimental import pallas as pl
from jax.experimental.pallas import tpu as pltpu

F32 = jnp.float32
BF16 = jnp.bfloat16

RMS_EPS = 1e-6
SB_HEADS = 8
SB_HEAD_DIM = 128
X_HEADS = 4
CONV_K = 3
LOG2_E = 1.4426950408889634
EXP2_UNDERFLOW = -151.0
DEAD_CARRY = -1e30

V7X_VMEM_BYTES = 64 * 1024 * 1024
LANES = 128
SUBLANES = 8
BF16_SUBLANES = 16
VMEM_LIMIT_BYTES = V7X_VMEM_BYTES - 4 * 1024 * 1024

MEM_KV_ROWS = 256
DENSE_ROWS = 1024
DENSE_SUB_ROWS = 256
SB_BLOCK = LANES
SB_TILES_PER_BODY = 2
SB_HEAD_GROUPS = 2


def _rms(x, g):
    ms = jnp.mean(x * x, axis=-1, keepdims=True)
    return (x * lax.rsqrt(ms + RMS_EPS)) * g


def _dot(a, b):
    return jnp.dot(a, b, preferred_element_type=F32)


def _dot_nt(a, b):
    return lax.dot_general(a, b, (((1,), (1,)), ((), ())), preferred_element_type=F32)


def _side_cast_specs(weights, nsteps, index_map):
    in_specs, out_specs, out_shapes = [], [], []
    for w in weights:
        rows, cols = w.shape
        slab = rows // nsteps
        assert slab * nsteps == rows and slab % BF16_SUBLANES == 0, (w.shape, nsteps)
        in_specs.append(pl.BlockSpec((slab, cols), index_map))
        out_specs.append(pl.BlockSpec((slab, cols), index_map))
        out_shapes.append(jax.ShapeDtypeStruct((rows, cols), BF16))
    return in_specs, out_specs, out_shapes


def _side_cast(in_refs, out_refs):
    for src, dst in zip(in_refs, out_refs):
        dst[...] = src[...].astype(BF16)


def _ffn_kernel(x_ref, g_ref, wgu_ref, wd_ref, gf_ref, *rest, d_ff, final_norm, sub):
    nside = (len(rest) - 1) // 2
    o_ref = rest[nside]
    _side_cast(rest[:nside], rest[nside + 1:])
    for r in range(x_ref.shape[0] // sub):
        rows = pl.ds(r * sub, sub)
        x = x_ref[rows, :]
        n = _rms(x, g_ref[...]).astype(BF16)
        gu = _dot(n, wgu_ref[...])
        gate = gu[:, :d_ff]
        up = gu[:, d_ff:]
        a = (gate * jax.nn.sigmoid(gate) * up).astype(BF16)
        y = x + 0.5 * _dot(a, wd_ref[...])
        if final_norm:
            y = _rms(y, gf_ref[...])
        o_ref[rows, :] = y


def _ffn(x2d, g, w_gu, w_down, g_final, *, final_norm, tm, sub, name, side=()):
    t, d = x2d.shape
    d_ff = w_down.shape[0]
    const = lambda i: (0, 0)
    tile = lambda i: (i, 0)
    side_in, side_out, side_shapes = _side_cast_specs(side, t // tm, tile)
    outs = pl.pallas_call(
        functools.partial(_ffn_kernel, d_ff=d_ff, final_norm=final_norm, sub=sub),
        out_shape=[jax.ShapeDtypeStruct((t, d), F32)] + side_shapes,
        grid=(t // tm,),
        in_specs=[
            pl.BlockSpec((tm, d), tile),
            pl.BlockSpec((1, d), const),
            pl.BlockSpec((d, 2 * d_ff), const, pipeline_mode=pl.Buffered(1)),
            pl.BlockSpec((d_ff, d), const, pipeline_mode=pl.Buffered(1)),
            pl.BlockSpec((1, d), const),
        ] + side_in,
        out_specs=[pl.BlockSpec((tm, d), tile)] + side_out,
        compiler_params=pltpu.CompilerParams(
            dimension_semantics=("arbitrary",), vmem_limit_bytes=VMEM_LIMIT_BYTES),
        name=name,
    )(x2d, g, w_gu, w_down, g_final, *side)
    return outs[0], outs[1:]


def _mixer_in_kernel(h_ref, g_ref, win_ref, bg_ref, cw_ref, wco_ref,
                     qkv_ref, yc_ref, gsb_ref, carry_ref, *, d, sub):
    j = pl.program_id(1)

    @pl.when(j == 0)
    def _():
        carry_ref[...] = jnp.zeros_like(carry_ref)

    cw = cw_ref[...]
    prev = carry_ref[...]
    row = lax.broadcasted_iota(jnp.int32, (sub, d), 0)
    for r in range(h_ref.shape[1] // sub):
        rows = pl.ds(r * sub, sub)
        u = _rms(h_ref[0, rows, :], g_ref[...]).astype(BF16)
        conv_in = _dot(u, win_ref[:, 0:3 * d])
        qkv_ref[0, rows, :] = _dot(u, win_ref[:, 3 * d:6 * d]).astype(BF16)
        gates = jax.nn.sigmoid(_dot(u, win_ref[:, 6 * d:8 * d]) + bg_ref[...])
        gsb_ref[0, rows, :] = gates[:, d:]

        p = conv_in[:, d:2 * d] * conv_in[:, 2 * d:3 * d]
        p1 = pltpu.roll(p, 1, 0)
        p1 = jnp.where(row == 0, prev[SUBLANES - 1:SUBLANES, :], p1)
        p2 = pltpu.roll(p, 2, 0)
        p2 = jnp.where(row == 0, prev[SUBLANES - 2:SUBLANES - 1, :], p2)
        p2 = jnp.where(row == 1, prev[SUBLANES - 1:SUBLANES, :], p2)
        y_conv = conv_in[:, 0:d] * (cw[0:1, :] * p2 + cw[1:2, :] * p1 + cw[2:3, :] * p)
        prev = p[sub - SUBLANES:, :]
        yc_ref[0, rows, :] = gates[:, :d] * _dot(y_conv.astype(BF16), wco_ref[...])
    carry_ref[...] = prev


def _mixer_in(h1, g, w_in, b_gate, conv_w, w_conv_out, *, tm, sub):
    b, s, d = h1.shape
    const = lambda bi, j: (0, 0)
    tile = lambda bi, j: (bi, j, 0)
    return pl.pallas_call(
        functools.partial(_mixer_in_kernel, d=d, sub=sub),
        out_shape=(
            jax.ShapeDtypeStruct((b, s, 3 * d), BF16),
            jax.ShapeDtypeStruct((b, s, d), F32),
            jax.ShapeDtypeStruct((b, s, d), F32),
        ),
        grid=(b, s // tm),
        in_specs=[
            pl.BlockSpec((1, tm, d), tile),
            pl.BlockSpec((1, d), const),
            pl.BlockSpec((d, 8 * d), const, pipeline_mode=pl.Buffered(1)),
            pl.BlockSpec((1, 2 * d), const),
            pl.BlockSpec((CONV_K, d), const),
            pl.BlockSpec((d, d), const, pipeline_mode=pl.Buffered(1)),
        ],
        out_specs=(
            pl.BlockSpec((1, tm, 3 * d), tile),
            pl.BlockSpec((1, tm, d), tile),
            pl.BlockSpec((1, tm, d), tile),
        ),
        scratch_shapes=[pltpu.VMEM((SUBLANES, d), F32)],
        compiler_params=pltpu.CompilerParams(
            dimension_semantics=("arbitrary", "arbitrary"), vmem_limit_bytes=VMEM_LIMIT_BYTES),
        name="mixer_in",
    )(h1, g, w_in, b_gate, conv_w, w_conv_out)


def _sb_attn_kernel(q_ref, k_ref, v_ref, *rest, blk, heads, tiles):
    nside = (len(rest) - 3) // 2
    o_ref = rest[nside]
    carry_ref, acc_ref = rest[-2:]
    _side_cast(rest[:nside], rest[nside + 1:-2])

    s = q_ref.shape[1]
    ngroups = s // (blk * tiles)
    hd = SB_HEAD_DIM
    c2 = hd ** -0.5 * LOG2_E

    row = lax.broadcasted_iota(jnp.int32, (blk, blk), 0)
    col = lax.broadcasted_iota(jnp.int32, (blk, blk), 1)
    causal = col < row
    tri_ones = jnp.concatenate(
        [causal.astype(BF16), jnp.ones((blk, blk), BF16)], axis=1)
    tri_ones2 = jnp.concatenate([tri_ones, tri_ones], axis=0)
    cols = [slice(g * hd, (g + 1) * hd) for g in range(heads)]
    half = blk // 2

    def step(i0, blocks, guarded):
        koffs, spent = [], []
        for n, _, _, _ in blocks:
            js = [i0 + t - n for t in range(tiles)]
            koffs.append([pl.multiple_of(jnp.maximum(j, 0) * blk, blk) for j in js])
            spent.append([j <= 0 for j in js])
        alive = {}
        for g0 in range(0, heads, heads // SB_HEAD_GROUPS):
            chains = [(t, g) for t in range(tiles)
                      for g in range(g0, g0 + heads // SB_HEAD_GROUPS)]
            slot = [t * heads + g for t, g in chains]
            z2s = [[_dot_nt(q_ref[0, pl.ds(pl.multiple_of((i0 + t) * blk + r0, half), nrows), cols[g]],
                            k_ref[0, pl.ds(koff[t], blk), cols[g]]) * c2 for t, g in chains]
                   for koff, (_, _, r0, nrows) in zip(koffs, blocks)]
            log_betas, rs = [], []
            for (_, diagonal, r0, nrows), z2_blk in zip(blocks, z2s):
                lb_blk, r_blk = [], []
                for z2 in z2_blk:
                    sp = jnp.log2(1.0 + jnp.exp2(-jnp.abs(z2)))
                    log_beta = jnp.minimum(z2, 0.0) - sp
                    log_1m = log_beta - z2
                    lb_blk.append(log_beta)
                    if diagonal:
                        log_1m = jnp.where(causal[r0:r0 + nrows], log_1m, 0.0)
                    hi = log_1m.astype(BF16)
                    lo = (log_1m - hi.astype(F32)).astype(BF16)
                    r_blk.append(_dot(jnp.concatenate([hi, lo], axis=1), tri_ones2))
                log_betas.append(lb_blk)
                rs.append(r_blk)
            carries = [{} for _ in chains]
            accs = [{} for _ in chains]
            for b, (_, diagonal, r0, nrows) in enumerate(blocks):
                halves = range(r0 // half, (r0 + nrows) // half)
                for c, (t, g) in enumerate(chains):
                    after = rs[b][c][:, :blk]
                    carry = rs[b][c][:, blk:]
                    if not diagonal:
                        old = jnp.concatenate(
                            [carries[c][h] if h in carries[c]
                             else carry_ref[slot[c], h * half:(h + 1) * half, :] for h in halves], axis=0)
                        after = after + old
                        carry = carry + old
                    a = jnp.exp2(log_betas[b][c] + after)
                    if diagonal:
                        a = jnp.where(causal[r0:r0 + nrows], a, 0.0)
                    pv = _dot(a.astype(BF16), v_ref[0, pl.ds(koffs[b][t], blk), cols[g]])
                    if t < tiles - 1 or not guarded:
                        carry = jnp.where(spent[b][t], DEAD_CARRY, carry)
                    for k, h in enumerate(halves):
                        rows = slice(k * half, (k + 1) * half)
                        if diagonal:
                            accs[c][h] = pv[rows]
                        elif h in accs[c]:
                            accs[c][h] = accs[c][h] + pv[rows]
                        else:
                            accs[c][h] = acc_ref[slot[c], h * half:(h + 1) * half, :] + pv[rows]
                        carries[c][h] = carry[rows]
            for c in range(len(chains)):
                for h in carries[c]:
                    acc_ref[slot[c], h * half:(h + 1) * half, :] = accs[c][h]
                    carry_ref[slot[c], h * half:(h + 1) * half, :] = carries[c][h]
                    alive[h] = carries[c][h] if h not in alive else jnp.maximum(alive[h], carries[c][h])
        return {h: jnp.max(m) for h, m in alive.items()}

    def q_group(grp, _):
        i0 = grp * tiles
        last = i0 + tiles - 1
        alive = step(i0, ((0, True, 0, blk), (1, False, 0, blk), (2, False, 0, half)), False)

        def walk(h, n_start, alive_h):
            def cond(st):
                return jnp.logical_and(st[0] <= last, st[1] >= EXP2_UNDERFLOW)

            def body(st):
                return st[0] + 1, step(i0, ((st[0], False, h * half, half),), True)[h]

            lax.while_loop(cond, body, (n_start, alive_h))

        walk(1, 2, alive[1])
        walk(0, 3, alive[0])
        for t in range(tiles):
            rows = pl.ds(pl.multiple_of((i0 + t) * blk, blk), blk)
            for g in range(heads):
                o_ref[0, rows, cols[g]] = acc_ref[t * heads + g].astype(o_ref.dtype)
        return 0

    lax.fori_loop(0, ngroups, q_group, 0)


def _sb_attn(qkv, *, blk, heads, tiles, side=()):
    b, s, d3 = qkv.shape
    d = d3 // 3
    hd = SB_HEAD_DIM
    w = heads * hd
    ng = d // w
    side_in, side_out, side_shapes = _side_cast_specs(
        side, b * ng, lambda bi, h: (bi * ng + h, 0))
    outs = pl.pallas_call(
        functools.partial(_sb_attn_kernel, blk=blk, heads=heads, tiles=tiles),
        out_shape=[jax.ShapeDtypeStruct((b, s, d), BF16)] + side_shapes,
        grid=(b, ng),
        in_specs=[
            pl.BlockSpec((1, s, w), lambda bi, h: (bi, 0, h)),
            pl.BlockSpec((1, s, w), lambda bi, h: (bi, 0, ng + h)),
            pl.BlockSpec((1, s, w), lambda bi, h: (bi, 0, 2 * ng + h)),
        ] + side_in,
        out_specs=[pl.BlockSpec((1, s, w), lambda bi, h: (bi, 0, h))] + side_out,
        scratch_shapes=[
            pltpu.VMEM((tiles * heads, blk, blk), F32),
            pltpu.VMEM((tiles * heads, blk, hd), F32),
        ],
        compiler_params=pltpu.CompilerParams(
            dimension_semantics=("arbitrary", "arbitrary"), vmem_limit_bytes=VMEM_LIMIT_BYTES),
        name="sb_attn",
    )(qkv, qkv, qkv, *side)
    return outs[0], outs[1:]


def _mem_kv_kernel(m_ref, g_ref, w_ref, *rest):
    nside = (len(rest) - 1) // 2
    o_ref = rest[nside]
    _side_cast(rest[:nside], rest[nside + 1:])
    mn = _rms(m_ref[...], g_ref[...]).astype(BF16)
    o_ref[...] = _dot(mn, w_ref[...].astype(BF16)).astype(BF16)


def _mem_kv(mem, g, w_ckv, *, tm, side=()):
    b, m, d = mem.shape
    const = lambda i: (0, 0)
    tile = lambda i: (i, 0)
    side_in, side_out, side_shapes = _side_cast_specs(side, b * m // tm, tile)
    outs = pl.pallas_call(
        _mem_kv_kernel,
        out_shape=[jax.ShapeDtypeStruct((b * m, 2 * d), BF16)] + side_shapes,
        grid=(b * m // tm,),
        in_specs=[
            pl.BlockSpec((tm, d), tile),
            pl.BlockSpec((1, d), const),
            pl.BlockSpec((d, 2 * d), const, pipeline_mode=pl.Buffered(1)),
        ] + side_in,
        out_specs=[pl.BlockSpec((tm, 2 * d), tile)] + side_out,
        compiler_params=pltpu.CompilerParams(
            dimension_semantics=("arbitrary",), vmem_limit_bytes=VMEM_LIMIT_BYTES),
        name="mem_kv",
    )(mem.reshape(b * m, d), g, w_ckv, *side)
    return outs[0].reshape(b, m, 2 * d), outs[1:]


def _mix_cross_kernel(h_ref, ysb_ref, yc_ref, gsb_ref, kx_ref, vx_ref,
                      wao_ref, wo_ref, g_ref, wcq_ref, wco_ref, o_ref, *, d, sub):
    hd = d // X_HEADS
    tiles = [pl.ds(r * sub, sub) for r in range(h_ref.shape[1] // sub)]
    heads = [slice(hh * hd, (hh + 1) * hd) for hh in range(X_HEADS)]
    att = [_dot(ysb_ref[0, t, :], wao_ref[...]) for t in tiles]
    merged = [(yc_ref[0, t, :] + gsb_ref[0, t, :] * a).astype(BF16) for t, a in zip(tiles, att)]
    h2 = [h_ref[0, t, :] + _dot(m, wo_ref[...]) for t, m in zip(tiles, merged)]
    hn = [_rms(x, g_ref[...]).astype(BF16) for x in h2]
    qx = [_dot(x, wcq_ref[...]).astype(BF16) for x in hn]
    q_all = jnp.concatenate(qx, axis=0)
    sc = [_dot_nt(q_all[:, sl], kx_ref[0, :, sl]) * (hd ** -0.5) for sl in heads]
    o_h = []
    for s_h, sl in zip(sc, heads):
        e = jnp.exp(s_h - jnp.max(s_h, axis=-1, keepdims=True))
        p = e / jnp.sum(e, axis=-1, keepdims=True)
        o_h.append(_dot(p.astype(BF16), vx_ref[0, :, sl]).astype(BF16))
    o_all = jnp.concatenate(o_h, axis=1)
    for r, (t, x) in enumerate(zip(tiles, h2)):
        o_ref[0, t, :] = x + _dot(o_all[r * sub:(r + 1) * sub], wco_ref[...])


def _mix_cross(h1, y_sb, yc, g_sb, kv, w_attn_out, w_o, g_cross, w_cq, w_co, *, tm, sub):
    b, s, d = h1.shape
    m = kv.shape[1]
    const = lambda bi, j: (0, 0)
    tile = lambda bi, j: (bi, j, 0)
    wspec = pl.BlockSpec((d, d), const, pipeline_mode=pl.Buffered(1))
    return pl.pallas_call(
        functools.partial(_mix_cross_kernel, d=d, sub=sub),
        out_shape=jax.ShapeDtypeStruct((b, s, d), F32),
        grid=(b, s // tm),
        in_specs=[
            pl.BlockSpec((1, tm, d), tile),
            pl.BlockSpec((1, tm, d), tile),
            pl.BlockSpec((1, tm, d), tile),
            pl.BlockSpec((1, tm, d), tile),
            pl.BlockSpec((1, m, d), lambda bi, j: (bi, 0, 0)),
            pl.BlockSpec((1, m, d), lambda bi, j: (bi, 0, 1)),
            wspec, wspec,
            pl.BlockSpec((1, d), const),
            wspec, wspec,
        ],
        out_specs=pl.BlockSpec((1, tm, d), tile),
        compiler_params=pltpu.CompilerParams(
            dimension_semantics=("arbitrary", "arbitrary"), vmem_limit_bytes=VMEM_LIMIT_BYTES),
        name="mix_cross",
    )(h1, y_sb, yc, g_sb, kv, kv, w_attn_out, w_o, g_cross, w_cq, w_co)


def kernel(x, mem, g_ffn1, w_ffn1_gu, w_ffn1_down, g_mix, w_in, b_gate, conv_w,
           w_conv_out, w_attn_out, w_o, g_cross, g_mem, w_cq, w_ckv, w_co,
           g_ffn2, w_ffn2_gu, w_ffn2_down, g_final):
    b, s, d = x.shape
    assert d == SB_HEADS * SB_HEAD_DIM and d % X_HEADS == 0, d
    assert s % DENSE_ROWS == 0 and s % (SB_BLOCK * SB_TILES_PER_BODY) == 0, s
    assert (b * mem.shape[1]) % MEM_KV_ROWS == 0, mem.shape
    row = lambda v: v.reshape(1, -1).astype(F32)

    dense = dict(tm=DENSE_ROWS, sub=DENSE_SUB_ROWS)
    kv, (w_ffn1_gu_b, w_ffn1_down_b) = _mem_kv(
        mem, row(g_mem), w_ckv, tm=MEM_KV_ROWS, side=(w_ffn1_gu, w_ffn1_down))
    h1, (w_in_b, w_conv_out_b, w_attn_out_b, w_o_b, w_cq_b, w_co_b) = _ffn(
        x.reshape(b * s, d), row(g_ffn1), w_ffn1_gu_b, w_ffn1_down_b, row(g_final),
        final_norm=False, name="ffn1", side=(w_in, w_conv_out, w_attn_out, w_o, w_cq, w_co),
        **dense)
    h1 = h1.reshape(b, s, d)
    qkv, yc, g_sb = _mixer_in(h1, row(g_mix), w_in_b, row(b_gate), conv_w.astype(F32),
                              w_conv_out_b, **dense)
    y_sb, (w_ffn2_gu_b, w_ffn2_down_b) = _sb_attn(
        qkv, blk=SB_BLOCK, heads=SB_HEADS, tiles=SB_TILES_PER_BODY,
        side=(w_ffn2_gu, w_ffn2_down))
    h3 = _mix_cross(h1, y_sb, yc, g_sb, kv, w_attn_out_b, w_o_b, row(g_cross),
                    w_cq_b, w_co_b, **dense)
    out, _ = _ffn(h3.reshape(b * s, d), row(g_ffn2), w_ffn2_gu_b, w_ffn2_down_b, row(g_final),
                  final_norm=True, name="ffn2", **dense)
    return out.reshape(b, s, d)
```

```python
import functools

import jax
import jax.numpy as jnp
from jax import lax
from jax.experimental import pallas as pl
from jax.experimental.pallas import tpu as pltpu

F32 = jnp.float32
BF16 = jnp.bfloat16

RMS_EPS = 1e-6
SB_HEADS = 8
SB_HEAD_DIM = 128
X_HEADS = 4
CONV_K = 3
LOG2_E = 1.4426950408889634
EXP2_UNDERFLOW = -151.0
DEAD_CARRY = -1e30

V7X_VMEM_BYTES = 64 * 1024 * 1024
LANES = 128
SUBLANES = 8
BF16_SUBLANES = 16
VMEM_LIMIT_BYTES = V7X_VMEM_BYTES - 4 * 1024 * 1024

MEM_KV_ROWS = 256
DENSE_ROWS = 1024
DENSE_SUB_ROWS = 256
SB_BLOCK = LANES
SB_TILES_PER_BODY = 2


def _rms(x, g):
    ms = jnp.mean(x * x, axis=-1, keepdims=True)
    return (x * lax.rsqrt(ms + RMS_EPS)) * g


def _dot(a, b):
    return jnp.dot(a, b, preferred_element_type=F32)


def _dot_nt(a, b):
    return lax.dot_general(a, b, (((1,), (1,)), ((), ())), preferred_element_type=F32)


def _side_cast_specs(weights, nsteps, index_map):
    in_specs, out_specs, out_shapes = [], [], []
    for w in weights:
        rows, cols = w.shape
        slab = rows // nsteps
        assert slab * nsteps == rows and slab % BF16_SUBLANES == 0, (w.shape, nsteps)
        in_specs.append(pl.BlockSpec((slab, cols), index_map))
        out_specs.append(pl.BlockSpec((slab, cols), index_map))
        out_shapes.append(jax.ShapeDtypeStruct((rows, cols), BF16))
    return in_specs, out_specs, out_shapes


def _side_cast(in_refs, out_refs):
    for src, dst in zip(in_refs, out_refs):
        dst[...] = src[...].astype(BF16)


def _ffn_kernel(x_ref, g_ref, wgu_ref, wd_ref, gf_ref, *rest, d_ff, final_norm, sub):
    nside = (len(rest) - 1) // 2
    o_ref = rest[nside]
    _side_cast(rest[:nside], rest[nside + 1:])
    wd = wd_ref[...]
    if wd.dtype != BF16:
        wd = wd.astype(BF16)
    for r in range(x_ref.shape[0] // sub):
        rows = pl.ds(r * sub, sub)
        x = x_ref[rows, :]
        n = _rms(x, g_ref[...]).astype(BF16)
        gu = _dot(n, wgu_ref[...])
        gate = gu[:, :d_ff]
        up = gu[:, d_ff:]
        a = (gate * jax.nn.sigmoid(gate) * up).astype(BF16)
        y = x + 0.5 * _dot(a, wd)
        if final_norm:
            y = _rms(y, gf_ref[...])
        o_ref[rows, :] = y


def _ffn(x2d, g, w_gu, w_down, g_final, *, final_norm, tm, sub, name, side=()):
    t, d = x2d.shape
    d_ff = w_down.shape[0]
    const = lambda i: (0, 0)
    tile = lambda i: (i, 0)
    side_in, side_out, side_shapes = _side_cast_specs(side, t // tm, tile)
    outs = pl.pallas_call(
        functools.partial(_ffn_kernel, d_ff=d_ff, final_norm=final_norm, sub=sub),
        out_shape=[jax.ShapeDtypeStruct((t, d), F32)] + side_shapes,
        grid=(t // tm,),
        in_specs=[
            pl.BlockSpec((tm, d), tile),
            pl.BlockSpec((1, d), const),
            pl.BlockSpec((d, 2 * d_ff), const, pipeline_mode=pl.Buffered(1)),
            pl.BlockSpec((d_ff, d), const, pipeline_mode=pl.Buffered(1)),
            pl.BlockSpec((1, d), const),
        ] + side_in,
        out_specs=[pl.BlockSpec((tm, d), tile)] + side_out,
        compiler_params=pltpu.CompilerParams(
            dimension_semantics=("arbitrary",), vmem_limit_bytes=VMEM_LIMIT_BYTES),
        name=name,
    )(x2d, g, w_gu, w_down, g_final, *side)
    return outs[0], outs[1:]


def _mixer_in_kernel(h_ref, g_ref, win_ref, bg_ref, cw_ref, wco_ref,
                     qkv_ref, yc_ref, gsb_ref, carry_ref, *, d, sub):
    j = pl.program_id(1)

    @pl.when(j == 0)
    def _():
        carry_ref[...] = jnp.zeros_like(carry_ref)

    cw = cw_ref[...]
    prev = carry_ref[...]
    row = lax.broadcasted_iota(jnp.int32, (sub, d), 0)
    for r in range(h_ref.shape[1] // sub):
        rows = pl.ds(r * sub, sub)
        u = _rms(h_ref[0, rows, :], g_ref[...]).astype(BF16)
        conv_in = _dot(u, win_ref[:, 0:3 * d])
        qkv_ref[0, rows, :] = _dot(u, win_ref[:, 3 * d:6 * d]).astype(BF16)
        gates = jax.nn.sigmoid(_dot(u, win_ref[:, 6 * d:8 * d]) + bg_ref[...])
        gsb_ref[0, rows, :] = gates[:, d:]

        p = conv_in[:, d:2 * d] * conv_in[:, 2 * d:3 * d]
        p1 = pltpu.roll(p, 1, 0)
        p1 = jnp.where(row == 0, prev[SUBLANES - 1:SUBLANES, :], p1)
        p2 = pltpu.roll(p, 2, 0)
        p2 = jnp.where(row == 0, prev[SUBLANES - 2:SUBLANES - 1, :], p2)
        p2 = jnp.where(row == 1, prev[SUBLANES - 1:SUBLANES, :], p2)
        y_conv = conv_in[:, 0:d] * (cw[0:1, :] * p2 + cw[1:2, :] * p1 + cw[2:3, :] * p)
        prev = p[sub - SUBLANES:, :]
        yc_ref[0, rows, :] = gates[:, :d] * _dot(y_conv.astype(BF16), wco_ref[...])
    carry_ref[...] = prev


def _mixer_in(h1, g, w_in, b_gate, conv_w, w_conv_out, *, tm, sub):
    b, s, d = h1.shape
    const = lambda bi, j: (0, 0)
    tile = lambda bi, j: (bi, j, 0)
    return pl.pallas_call(
        functools.partial(_mixer_in_kernel, d=d, sub=sub),
        out_shape=(
            jax.ShapeDtypeStruct((b, s, 3 * d), BF16),
            jax.ShapeDtypeStruct((b, s, d), F32),
            jax.ShapeDtypeStruct((b, s, d), F32),
        ),
        grid=(b, s // tm),
        in_specs=[
            pl.BlockSpec((1, tm, d), tile),
            pl.BlockSpec((1, d), const),
            pl.BlockSpec((d, 8 * d), const, pipeline_mode=pl.Buffered(1)),
            pl.BlockSpec((1, 2 * d), const),
            pl.BlockSpec((CONV_K, d), const),
            pl.BlockSpec((d, d), const, pipeline_mode=pl.Buffered(1)),
        ],
        out_specs=(
            pl.BlockSpec((1, tm, 3 * d), tile),
            pl.BlockSpec((1, tm, d), tile),
            pl.BlockSpec((1, tm, d), tile),
        ),
        scratch_shapes=[pltpu.VMEM((SUBLANES, d), F32)],
        compiler_params=pltpu.CompilerParams(
            dimension_semantics=("arbitrary", "arbitrary"), vmem_limit_bytes=VMEM_LIMIT_BYTES),
        name="mixer_in",
    )(h1, g, w_in, b_gate, conv_w, w_conv_out)


def _sb_attn_kernel(q_ref, k_ref, v_ref, *rest, blk, heads, tiles):
    nside = (len(rest) - 3) // 2
    o_ref = rest[nside]
    carry_ref, acc_ref = rest[-2:]
    _side_cast(rest[:nside], rest[nside + 1:-2])

    s = q_ref.shape[1]
    ngroups = s // (blk * tiles)
    hd = SB_HEAD_DIM
    c2 = hd ** -0.5 * LOG2_E

    row = lax.broadcasted_iota(jnp.int32, (blk, blk), 0)
    col = lax.broadcasted_iota(jnp.int32, (blk, blk), 1)
    causal = col < row
    tri_ones = jnp.concatenate(
        [causal.astype(BF16), jnp.ones((blk, blk), BF16)], axis=1)
    tri_ones2 = jnp.concatenate([tri_ones, tri_ones], axis=0)
    cols = [slice(g * hd, (g + 1) * hd) for g in range(heads)]
    half = blk // 2

    def step(i0, blocks, guarded):
        chains = [(t, g) for t in range(tiles) for g in range(heads)]
        koffs, spent = [], []
        for n, _, _, _ in blocks:
            js = [i0 + t - n for t in range(tiles)]
            koffs.append([pl.multiple_of(jnp.maximum(j, 0) * blk, blk) for j in js])
            spent.append([j <= 0 for j in js])
        z2s = [[_dot_nt(q_ref[0, pl.ds(pl.multiple_of((i0 + t) * blk + r0, half), nrows), cols[g]],
                        k_ref[0, pl.ds(koff[t], blk), cols[g]]) * c2 for t, g in chains]
               for koff, (_, _, r0, nrows) in zip(koffs, blocks)]
        log_betas, rs = [], []
        for (_, diagonal, r0, nrows), z2_blk in zip(blocks, z2s):
            lb_blk, r_blk = [], []
            for z2 in z2_blk:
                sp = jnp.log2(1.0 + jnp.exp2(-jnp.abs(z2)))
                log_beta = jnp.minimum(z2, 0.0) - sp
                log_1m = log_beta - z2
                lb_blk.append(log_beta)
                if diagonal:
                    log_1m = jnp.where(causal[r0:r0 + nrows], log_1m, 0.0)
                hi = log_1m.astype(BF16)
                lo = (log_1m - hi.astype(F32)).astype(BF16)
                r_blk.append(_dot(jnp.concatenate([hi, lo], axis=1), tri_ones2))
            log_betas.append(lb_blk)
            rs.append(r_blk)
        carries = [{} for _ in chains]
        accs = [{} for _ in chains]
        for b, (_, diagonal, r0, nrows) in enumerate(blocks):
            halves = range(r0 // half, (r0 + nrows) // half)
            for c, (t, g) in enumerate(chains):
                after = rs[b][c][:, :blk]
                carry = rs[b][c][:, blk:]
                if not diagonal:
                    old = jnp.concatenate(
                        [carries[c][h] if h in carries[c]
                         else carry_ref[c, h * half:(h + 1) * half, :] for h in halves], axis=0)
                    after = after + old
                    carry = carry + old
                a = jnp.exp2(log_betas[b][c] + after)
                if diagonal:
                    a = jnp.where(causal[r0:r0 + nrows], a, 0.0)
                pv = _dot(a.astype(BF16), v_ref[0, pl.ds(koffs[b][t], blk), cols[g]])
                if t < tiles - 1 or not guarded:
                    carry = jnp.where(spent[b][t], DEAD_CARRY, carry)
                for k, h in enumerate(halves):
                    rows = slice(k * half, (k + 1) * half)
                    if diagonal:
                        accs[c][h] = pv[rows]
                    elif h in accs[c]:
                        accs[c][h] = accs[c][h] + pv[rows]
                    else:
                        accs[c][h] = acc_ref[c, h * half:(h + 1) * half, :] + pv[rows]
                    carries[c][h] = carry[rows]
        alive = {}
        for c in range(len(chains)):
            for h in carries[c]:
                acc_ref[c, h * half:(h + 1) * half, :] = accs[c][h]
                carry_ref[c, h * half:(h + 1) * half, :] = carries[c][h]
                alive[h] = carries[c][h] if h not in alive else jnp.maximum(alive[h], carries[c][h])
        return {h: jnp.max(m) for h, m in alive.items()}

    def q_group(grp, _):
        i0 = grp * tiles
        last = i0 + tiles - 1
        alive = step(i0, ((0, True, 0, blk), (1, False, 0, blk), (2, False, 0, half)), False)

        def walk(h, n_start, alive_h):
            def cond(st):
                return jnp.logical_and(st[0] <= last, st[1] >= EXP2_UNDERFLOW)

            def body(st):
                return st[0] + 1, step(i0, ((st[0], False, h * half, half),), True)[h]

            lax.while_loop(cond, body, (n_start, alive_h))

        walk(1, 2, alive[1])
        walk(0, 3, alive[0])
        for t in range(tiles):
            rows = pl.ds(pl.multiple_of((i0 + t) * blk, blk), blk)
            for g in range(heads):
                o_ref[0, rows, cols[g]] = acc_ref[t * heads + g].astype(o_ref.dtype)
        return 0

    lax.fori_loop(0, ngroups, q_group, 0)


def _sb_attn(qkv, *, blk, heads, tiles, side=()):
    b, s, d3 = qkv.shape
    d = d3 // 3
    hd = SB_HEAD_DIM
    w = heads * hd
    ng = d // w
    side_in, side_out, side_shapes = _side_cast_specs(
        side, b * ng, lambda bi, h: (bi * ng + h, 0))
    outs = pl.pallas_call(
        functools.partial(_sb_attn_kernel, blk=blk, heads=heads, tiles=tiles),
        out_shape=[jax.ShapeDtypeStruct((b, s, d), BF16)] + side_shapes,
        grid=(b, ng),
        in_specs=[
            pl.BlockSpec((1, s, w), lambda bi, h: (bi, 0, h)),
            pl.BlockSpec((1, s, w), lambda bi, h: (bi, 0, ng + h)),
            pl.BlockSpec((1, s, w), lambda bi, h: (bi, 0, 2 * ng + h)),
        ] + side_in,
        out_specs=[pl.BlockSpec((1, s, w), lambda bi, h: (bi, 0, h))] + side_out,
        scratch_shapes=[
            pltpu.VMEM((tiles * heads, blk, blk), F32),
            pltpu.VMEM((tiles * heads, blk, hd), F32),
        ],
        compiler_params=pltpu.CompilerParams(
            dimension_semantics=("arbitrary", "arbitrary"), vmem_limit_bytes=VMEM_LIMIT_BYTES),
        name="sb_attn",
    )(qkv, qkv, qkv, *side)
    return outs[0], outs[1:]


def _mem_kv_kernel(m_ref, g_ref, w_ref, *rest):
    nside = (len(rest) - 1) // 2
    o_ref = rest[nside]
    _side_cast(rest[:nside], rest[nside + 1:])
    mn = _rms(m_ref[...], g_ref[...]).astype(BF16)
    o_ref[...] = _dot(mn, w_ref[...].astype(BF16)).astype(BF16)


def _mem_kv(mem, g, w_ckv, *, tm, side=()):
    b, m, d = mem.shape
    const = lambda i: (0, 0)
    tile = lambda i: (i, 0)
    side_in, side_out, side_shapes = _side_cast_specs(side, b * m // tm, tile)
    outs = pl.pallas_call(
        _mem_kv_kernel,
        out_shape=[jax.ShapeDtypeStruct((b * m, 2 * d), BF16)] + side_shapes,
        grid=(b * m // tm,),
        in_specs=[
            pl.BlockSpec((tm, d), tile),
            pl.BlockSpec((1, d), const),
            pl.BlockSpec((d, 2 * d), const, pipeline_mode=pl.Buffered(1)),
        ] + side_in,
        out_specs=[pl.BlockSpec((tm, 2 * d), tile)] + side_out,
        compiler_params=pltpu.CompilerParams(
            dimension_semantics=("arbitrary",), vmem_limit_bytes=VMEM_LIMIT_BYTES),
        name="mem_kv",
    )(mem.reshape(b * m, d), g, w_ckv, *side)
    return outs[0].reshape(b, m, 2 * d), outs[1:]


def _mix_cross_kernel(h_ref, ysb_ref, yc_ref, gsb_ref, kx_ref, vx_ref,
                      wao_ref, wo_ref, g_ref, wcq_ref, wco_ref, o_ref, *, d, sub):
    hd = d // X_HEADS
    tiles = [pl.ds(r * sub, sub) for r in range(h_ref.shape[1] // sub)]
    heads = [slice(hh * hd, (hh + 1) * hd) for hh in range(X_HEADS)]
    att = [_dot(ysb_ref[0, t, :], wao_ref[...]) for t in tiles]
    merged = [(yc_ref[0, t, :] + gsb_ref[0, t, :] * a).astype(BF16) for t, a in zip(tiles, att)]
    h2 = [h_ref[0, t, :] + _dot(m, wo_ref[...]) for t, m in zip(tiles, merged)]
    hn = [_rms(x, g_ref[...]).astype(BF16) for x in h2]
    qx = [_dot(x, wcq_ref[...]).astype(BF16) for x in hn]
    q_all = jnp.concatenate(qx, axis=0)
    sc = [_dot_nt(q_all[:, sl], kx_ref[0, :, sl]) * (hd ** -0.5) for sl in heads]
    o_h = []
    for s_h, sl in zip(sc, heads):
        e = jnp.exp(s_h - jnp.max(s_h, axis=-1, keepdims=True))
        p = e / jnp.sum(e, axis=-1, keepdims=True)
        o_h.append(_dot(p.astype(BF16), vx_ref[0, :, sl]).astype(BF16))
    o_all = jnp.concatenate(o_h, axis=1)
    for r, (t, x) in enumerate(zip(tiles, h2)):
        o_ref[0, t, :] = x + _dot(o_all[r * sub:(r + 1) * sub], wco_ref[...])


def _mix_cross(h1, y_sb, yc, g_sb, kv, w_attn_out, w_o, g_cross, w_cq, w_co, *, tm, sub):
    b, s, d = h1.shape
    m = kv.shape[1]
    const = lambda bi, j: (0, 0)
    tile = lambda bi, j: (bi, j, 0)
    wspec = pl.BlockSpec((d, d), const, pipeline_mode=pl.Buffered(1))
    return pl.pallas_call(
        functools.partial(_mix_cross_kernel, d=d, sub=sub),
        out_shape=jax.ShapeDtypeStruct((b, s, d), F32),
        grid=(b, s // tm),
        in_specs=[
            pl.BlockSpec((1, tm, d), tile),
            pl.BlockSpec((1, tm, d), tile),
            pl.BlockSpec((1, tm, d), tile),
            pl.BlockSpec((1, tm, d), tile),
            pl.BlockSpec((1, m, d), lambda bi, j: (bi, 0, 0)),
            pl.BlockSpec((1, m, d), lambda bi, j: (bi, 0, 1)),
            wspec, wspec,
            pl.BlockSpec((1, d), const),
            wspec, wspec,
        ],
        out_specs=pl.BlockSpec((1, tm, d), tile),
        compiler_params=pltpu.CompilerParams(
            dimension_semantics=("arbitrary", "arbitrary"), vmem_limit_bytes=VMEM_LIMIT_BYTES),
        name="mix_cross",
    )(h1, y_sb, yc, g_sb, kv, kv, w_attn_out, w_o, g_cross, w_cq, w_co)


def kernel(x, mem, g_ffn1, w_ffn1_gu, w_ffn1_down, g_mix, w_in, b_gate, conv_w,
           w_conv_out, w_attn_out, w_o, g_cross, g_mem, w_cq, w_ckv, w_co,
           g_ffn2, w_ffn2_gu, w_ffn2_down, g_final):
    b, s, d = x.shape
    assert d == SB_HEADS * SB_HEAD_DIM and d % X_HEADS == 0, d
    assert s % DENSE_ROWS == 0 and s % (SB_BLOCK * SB_TILES_PER_BODY) == 0, s
    assert (b * mem.shape[1]) % MEM_KV_ROWS == 0, mem.shape
    row = lambda v: v.reshape(1, -1).astype(F32)

    dense = dict(tm=DENSE_ROWS, sub=DENSE_SUB_ROWS)
    kv, (w_ffn1_gu_b,) = _mem_kv(
        mem, row(g_mem), w_ckv, tm=MEM_KV_ROWS, side=(w_ffn1_gu,))
    h1, (w_in_b, w_conv_out_b, w_attn_out_b, w_o_b, w_cq_b, w_co_b) = _ffn(
        x.reshape(b * s, d), row(g_ffn1), w_ffn1_gu_b, w_ffn1_down, row(g_final),
        final_norm=False, name="ffn1", side=(w_in, w_conv_out, w_attn_out, w_o, w_cq, w_co),
        **dense)
    h1 = h1.reshape(b, s, d)
    qkv, yc, g_sb = _mixer_in(h1, row(g_mix), w_in_b, row(b_gate), conv_w.astype(F32),
                              w_conv_out_b, **dense)
    y_sb, (w_ffn2_gu_b, w_ffn2_down_b) = _sb_attn(
        qkv, blk=SB_BLOCK, heads=SB_HEADS, tiles=SB_TILES_PER_BODY,
        side=(w_ffn2_gu, w_ffn2_down))
    h3 = _mix_cross(h1, y_sb, yc, g_sb, kv, w_attn_out_b, w_o_b, row(g_cross),
                    w_cq_b, w_co_b, **dense)
    out, _ = _ffn(h3.reshape(b * s, d), row(g_ffn2), w_ffn2_gu_b, w_ffn2_down_b, row(g_final),
                  final_norm=True, name="ffn2", **dense)
    return out.reshape(b, s, d)
```
